```python
import math
import jax, jax.numpy as jnp
from jax import lax
import numpy as np

D_MODEL = 1024
BATCH = 1
SEQ = 16384
DEPTH = 1
DEC_BATCH = 8
DEC_SEQ = 32
PAST_LEN = 2048

CHUNK = 64
N_MEM = 256
MLA_HEADS = 8
Q_LORA = 384
KV_LORA = 256
QK_NOPE = 64
QK_ROPE = 32
V_HEAD = 64
MLA_WIDTH = MLA_HEADS * V_HEAD
MLA_SCALE = (QK_NOPE + QK_ROPE) ** -0.5
ROPE_THETA = 10000.0
Q_BLOCK = 128
LRU_WIDTH = 512
LRU_BLOCKS = 8
LRU_BLOCK = LRU_WIDTH // LRU_BLOCKS
CONV_W = 4
LRU_C = 8.0
MIX_WIDTH = MLA_WIDTH + LRU_WIDTH
IN_COLS = Q_LORA + KV_LORA + QK_ROPE + 2 * LRU_WIDTH
MEM_HEADS = 4
MEM_HEAD_DIM = D_MODEL // MEM_HEADS
D_FF = 2816
EPS = 1e-6

kernel_name = 'hybrid_mla_rglru_streaming_step'


def rmsnorm(x, g):
    xf = x.astype(jnp.float32)
    y = xf * lax.rsqrt(jnp.mean(xf * xf, axis=-1, keepdims=True) + EPS)
    return (y * g.astype(jnp.float32)).astype(x.dtype)


def swiglu(h, w1, w3, w2):
    return (jax.nn.silu(h @ w1) * (h @ w3)) @ w2


def rope_tables(pos):
    inv_freq = ROPE_THETA ** (-jnp.arange(0, QK_ROPE, 2, dtype=jnp.float32) / QK_ROPE)
    ang = pos.astype(jnp.float32)[:, None] * inv_freq[None, :]
    return jnp.cos(ang), jnp.sin(ang)


def apply_rope(x, cos, sin):
    x1, x2 = jnp.split(x, 2, axis=-1)
    shape = (cos.shape[0],) + (1,) * (x.ndim - 3) + (cos.shape[1],)
    c = cos.reshape(shape).astype(x.dtype)
    s = sin.reshape(shape).astype(x.dtype)
    return jnp.concatenate([x1 * c - x2 * s, x2 * c + x1 * s], axis=-1)


def mla_attend(q_nope, q_pe, k_nope, k_pe, v, mask):
    s = (jnp.einsum('bqhd,bkhd->bhqk', q_nope, k_nope)
         + jnp.einsum('bqhr,bkr->bhqk', q_pe, k_pe)).astype(jnp.float32) * MLA_SCALE
    if mask is not None:
        s = jnp.where(mask, s, -jnp.inf)
    p = jax.nn.softmax(s, axis=-1).astype(v.dtype)
    return jnp.einsum('bhqk,bkhd->bqhd', p, v)


def mla_prompt_attn(q_nope, q_pe, k_nope, k_pe, v):
    B, S = q_nope.shape[:2]
    nb = S // Q_BLOCK
    qn = q_nope.reshape(B, nb, Q_BLOCK, MLA_HEADS, QK_NOPE).transpose(1, 0, 2, 3, 4)
    qp = q_pe.reshape(B, nb, Q_BLOCK, MLA_HEADS, QK_ROPE).transpose(1, 0, 2, 3, 4)
    kchunk = jnp.arange(S) // CHUNK

    def block(args):
        qn_b, qp_b, bi = args
        qchunk = (bi * Q_BLOCK + jnp.arange(Q_BLOCK)) // CHUNK
        mask = kchunk[None, :] <= qchunk[:, None]
        return mla_attend(qn_b, qp_b, k_nope, k_pe, v, mask)

    out = lax.map(block, (qn, qp, jnp.arange(nb)))
    return out.transpose(1, 0, 2, 3, 4).reshape(B, S, MLA_WIDTH)


def causal_conv(xb, prev, w, b):
    T = xb.shape[1]
    xp = jnp.concatenate([prev, xb], axis=1)
    y = b + xp[:, 0:T] * w[0]
    for k in range(1, CONV_W):
        y = y + xp[:, k:k + T] * w[k]
    return y, xp[:, -(CONV_W - 1):]


def block_diag(x, w, b):
    B, T, _ = x.shape
    xh = x.reshape(B, T, LRU_BLOCKS, LRU_BLOCK)
    return (jnp.einsum('btgi,gij->btgj', xh, w) + b).reshape(B, T, LRU_WIDTH)


def rg_lru(xc, h0, w_a, b_a, w_x, b_x, lam):
    r = jax.nn.sigmoid(block_diag(xc, w_a, b_a).astype(jnp.float32))
    i = jax.nn.sigmoid(block_diag(xc, w_x, b_x).astype(jnp.float32))
    log_a = -LRU_C * r * jax.nn.softplus(-lam.astype(jnp.float32))
    a = jnp.exp(log_a)
    bterm = jnp.sqrt(-jnp.expm1(2.0 * log_a)) * i * xc.astype(jnp.float32)

    def comb(l, rr):
        a1, b1 = l
        a2, b2 = rr
        return a1 * a2, a2 * b1 + b2

    a_cum, b_cum = lax.associative_scan(comb, (a, bterm), axis=1)
    h = b_cum + a_cum * h0.astype(jnp.float32)[:, None, :]
    return h.astype(xc.dtype), h[:, -1].astype(h0.dtype)


def token_mix(h, pos, past_ckv, past_kpe, conv_prev, lru_h0, lw):
    B, T, _ = h.shape
    z = h @ lw['w_in']
    c_q, c_kv, k_pe, x_br, g_br = jnp.split(
        z, [Q_LORA, Q_LORA + KV_LORA, Q_LORA + KV_LORA + QK_ROPE,
            Q_LORA + KV_LORA + QK_ROPE + LRU_WIDTH], axis=-1)
    cos, sin = rope_tables(pos)
    c_kv = rmsnorm(c_kv, lw['kv_norm'])
    k_pe = apply_rope(k_pe, cos, sin)
    q = (rmsnorm(c_q, lw['q_norm']) @ lw['w_uq']).reshape(B, T, MLA_HEADS, QK_NOPE + QK_ROPE)
    q_nope = q[..., :QK_NOPE]
    q_pe = apply_rope(q[..., QK_NOPE:], cos, sin)
    if past_ckv is None:
        ckv_all, kpe_all = c_kv, k_pe
    else:
        ckv_all = jnp.concatenate([past_ckv, c_kv], axis=1)
        kpe_all = jnp.concatenate([past_kpe, k_pe], axis=1)
    S = ckv_all.shape[1]
    kv = (ckv_all @ lw['w_ukv']).reshape(B, S, MLA_HEADS, QK_NOPE + V_HEAD)
    k_nope, v = kv[..., :QK_NOPE], kv[..., QK_NOPE:]
    if past_ckv is None:
        attn = mla_prompt_attn(q_nope, q_pe, k_nope, kpe_all, v)
    else:
        attn = mla_attend(q_nope, q_pe, k_nope, kpe_all, v, None).reshape(B, T, MLA_WIDTH)
    xc, conv_state = causal_conv(x_br, conv_prev, lw['conv_w'], lw['conv_b'])
    h_lru, lru_state = rg_lru(xc, lru_h0, lw['lru_wa'], lw['lru_ba'], lw['lru_wx'],
                              lw['lru_bx'], lw['lru_lambda'])
    lru_out = jax.nn.gelu(g_br) * h_lru
    merged = jnp.concatenate([rmsnorm(attn, lw['attn_out_norm']),
                              rmsnorm(lru_out, lw['lru_out_norm'])], axis=-1)
    return merged @ lw['w_out'], (c_kv, k_pe, conv_state, lru_state)


def mem_kv(mem, g, w_mk, w_mv):
    m = rmsnorm(mem, g)
    B = m.shape[0]
    k = (m @ w_mk).reshape(B, N_MEM, MEM_HEADS, MEM_HEAD_DIM)
    v = (m @ w_mv).reshape(B, N_MEM, MEM_HEADS, MEM_HEAD_DIM)
    return k, v


def cross_attn(h, k, v, w_mq, w_mo):
    B, T, _ = h.shape
    q = (h @ w_mq).reshape(B, T, MEM_HEADS, MEM_HEAD_DIM)
    s = jnp.einsum('bqhd,bkhd->bhqk', q, k).astype(jnp.float32) * (MEM_HEAD_DIM ** -0.5)
    p = jax.nn.softmax(s, axis=-1).astype(v.dtype)
    o = jnp.einsum('bhqk,bkhd->bqhd', p, v).reshape(B, T, D_MODEL)
    return o @ w_mo


def layer(x, pos, mem_k, mem_v, past_ckv, past_kpe, conv_prev, lru_h0, lw):
    x = x + 0.5 * swiglu(rmsnorm(x, lw['ffn1_norm']), lw['ffn1_w1'], lw['ffn1_w3'], lw['ffn1_w2'])
    mix, st = token_mix(rmsnorm(x, lw['mix_norm']), pos, past_ckv, past_kpe, conv_prev, lru_h0, lw)
    x = x + mix
    x = x + cross_attn(rmsnorm(x, lw['xattn_norm']), mem_k, mem_v, lw['w_mq'], lw['w_mo'])
    x = x + 0.5 * swiglu(rmsnorm(x, lw['ffn2_norm']), lw['ffn2_w1'], lw['ffn2_w3'], lw['ffn2_w2'])
    return x, st


def setup_inputs(seed: int = 0) -> dict:
    key = jax.random.key(seed)
    ks = iter(jax.random.split(key, 64))
    L = DEPTH

    def nrm(shape, scale=1.0):
        return jax.random.normal(next(ks), shape, jnp.float32) * scale

    def gain(n):
        return 1.0 + nrm((L, n), 0.01)

    a0 = jax.random.uniform(next(ks), (L, LRU_WIDTH), jnp.float32, minval=0.9, maxval=0.999)
    return {
        'x_prompt': nrm((BATCH, SEQ, D_MODEL)),
        'x_sample': nrm((DEC_BATCH, DEC_SEQ, D_MODEL)),
        'mem_prompt': nrm((BATCH, N_MEM, D_MODEL)),
        'cache_mla_ckv': nrm((L, DEC_BATCH, PAST_LEN, KV_LORA)),
        'cache_mla_kpe': nrm((L, DEC_BATCH, PAST_LEN, QK_ROPE)),
        'state_conv': nrm((L, DEC_BATCH, CONV_W - 1, LRU_WIDTH)),
        'state_lru': nrm((L, DEC_BATCH, LRU_WIDTH), 0.5),
        'cache_mem_k': nrm((L, DEC_BATCH, N_MEM, MEM_HEADS, MEM_HEAD_DIM)),
        'cache_mem_v': nrm((L, DEC_BATCH, N_MEM, MEM_HEADS, MEM_HEAD_DIM)),
        'ffn1_norm': gain(D_MODEL),
        'ffn1_w1': nrm((L, D_MODEL, D_FF), D_MODEL ** -0.5),
        'ffn1_w3': nrm((L, D_MODEL, D_FF), D_MODEL ** -0.5),
        'ffn1_w2': nrm((L, D_FF, D_MODEL), D_FF ** -0.5),
        'mix_norm': gain(D_MODEL),
        'w_in': nrm((L, D_MODEL, IN_COLS), D_MODEL ** -0.5),
        'q_norm': gain(Q_LORA),
        'w_uq': nrm((L, Q_LORA, MLA_HEADS * (QK_NOPE + QK_ROPE)), Q_LORA ** -0.5),
        'kv_norm': gain(KV_LORA),
        'w_ukv': nrm((L, KV_LORA, MLA_HEADS * (QK_NOPE + V_HEAD)), KV_LORA ** -0.5),
        'conv_w': nrm((L, CONV_W, LRU_WIDTH), CONV_W ** -0.5),
        'conv_b': nrm((L, LRU_WIDTH), 0.01),
        'lru_wa': nrm((L, LRU_BLOCKS, LRU_BLOCK, LRU_BLOCK), LRU_BLOCK ** -0.5),
        'lru_ba': nrm((L, LRU_BLOCKS, LRU_BLOCK), 0.01),
        'lru_wx': nrm((L, LRU_BLOCKS, LRU_BLOCK, LRU_BLOCK), LRU_BLOCK ** -0.5),
        'lru_bx': nrm((L, LRU_BLOCKS, LRU_BLOCK), 0.01),
        'lru_lambda': jnp.log(a0) - jnp.log1p(-a0),
        'attn_out_norm': gain(MLA_WIDTH),
        'lru_out_norm': gain(LRU_WIDTH),
        'w_out': nrm((L, MIX_WIDTH, D_MODEL), MIX_WIDTH ** -0.5),
        'mem_norm': gain(D_MODEL),
        'xattn_norm': gain(D_MODEL),
        'w_mq': nrm((L, D_MODEL, D_MODEL), D_MODEL ** -0.5),
        'w_mk': nrm((L, D_MODEL, D_MODEL), D_MODEL ** -0.5),
        'w_mv': nrm((L, D_MODEL, D_MODEL), D_MODEL ** -0.5),
        'w_mo': nrm((L, D_MODEL, D_MODEL), D_MODEL ** -0.5),
        'ffn2_norm': gain(D_MODEL),
        'ffn2_w1': nrm((L, D_MODEL, D_FF), D_MODEL ** -0.5),
        'ffn2_w3': nrm((L, D_MODEL, D_FF), D_MODEL ** -0.5),
        'ffn2_w2': nrm((L, D_FF, D_MODEL), D_FF ** -0.5),
        'final_norm': 1.0 + nrm((D_MODEL,), 0.01),
    }


def reference(x_prompt, x_sample, mem_prompt, cache_mla_ckv, cache_mla_kpe, state_conv,
              state_lru, cache_mem_k, cache_mem_v, ffn1_norm, ffn1_w1, ffn1_w3, ffn1_w2,
              mix_norm, w_in, q_norm, w_uq, kv_norm, w_ukv, conv_w, conv_b, lru_wa, lru_ba,
              lru_wx, lru_bx, lru_lambda, attn_out_norm, lru_out_norm, w_out, mem_norm,
              xattn_norm, w_mq, w_mk, w_mv, w_mo, ffn2_norm, ffn2_w1, ffn2_w3, ffn2_w2,
              final_norm):
    B, S = x_prompt.shape[:2]
    past_len = cache_mla_ckv.shape[2]
    pos_p = jnp.arange(S)
    pos_s = past_len + jnp.arange(x_sample.shape[1])
    xp, xs = x_prompt, x_sample
    ckv_p, kpe_p, conv_p, lru_p, mk_p_l, mv_p_l = [], [], [], [], [], []
    ckv_s, kpe_s, conv_s, lru_s = [], [], [], []
    for l in range(DEPTH):
        lw = {
            'ffn1_norm': ffn1_norm[l], 'ffn1_w1': ffn1_w1[l], 'ffn1_w3': ffn1_w3[l],
            'ffn1_w2': ffn1_w2[l], 'mix_norm': mix_norm[l], 'w_in': w_in[l],
            'q_norm': q_norm[l], 'w_uq': w_uq[l], 'kv_norm': kv_norm[l], 'w_ukv': w_ukv[l],
            'conv_w': conv_w[l], 'conv_b': conv_b[l], 'lru_wa': lru_wa[l], 'lru_ba': lru_ba[l],
            'lru_wx': lru_wx[l], 'lru_bx': lru_bx[l], 'lru_lambda': lru_lambda[l],
            'attn_out_norm': attn_out_norm[l], 'lru_out_norm': lru_out_norm[l],
            'w_out': w_out[l], 'xattn_norm': xattn_norm[l], 'w_mq': w_mq[l], 'w_mo': w_mo[l],
            'ffn2_norm': ffn2_norm[l], 'ffn2_w1': ffn2_w1[l], 'ffn2_w3': ffn2_w3[l],
            'ffn2_w2': ffn2_w2[l],
        }
        mk_p, mv_p = mem_kv(mem_prompt, mem_norm[l], w_mk[l], w_mv[l])
        zero_conv = jnp.zeros((B, CONV_W - 1, LRU_WIDTH), xp.dtype)
        zero_lru = jnp.zeros((B, LRU_WIDTH), xp.dtype)
        xp, st_p = layer(xp, pos_p, mk_p, mv_p, None, None, zero_conv, zero_lru, lw)
        xs, st_s = layer(xs, pos_s, cache_mem_k[l], cache_mem_v[l], cache_mla_ckv[l],
                         cache_mla_kpe[l], state_conv[l], state_lru[l], lw)
        ckv_p.append(st_p[0]); kpe_p.append(st_p[1]); conv_p.append(st_p[2]); lru_p.append(st_p[3])
        mk_p_l.append(mk_p); mv_p_l.append(mv_p)
        ckv_s.append(st_s[0]); kpe_s.append(st_s[1]); conv_s.append(st_s[2]); lru_s.append(st_s[3])
    y_prompt = rmsnorm(xp, final_norm)
    y_sample = rmsnorm(xs, final_norm)
    return (y_prompt, y_sample,
            jnp.stack(ckv_p), jnp.stack(kpe_p), jnp.stack(conv_p), jnp.stack(lru_p),
            jnp.stack(mk_p_l), jnp.stack(mv_p_l),
            jnp.stack(ckv_s), jnp.stack(kpe_s), jnp.stack(conv_s), jnp.stack(lru_s))
```

```python
import functools

import jax
import jax.numpy as jnp
from jax import lax
from jax.experimental import pallas as pl
from jax.experimental.pallas import tpu as pltpu

F32 = jnp.float32
BF16 = jnp.bfloat16

D_MODEL = 1024
CHUNK = 64
N_MEM = 256
MLA_HEADS = 8
Q_LORA = 384
KV_LORA = 256
QK_NOPE = 64
QK_ROPE = 32
V_HEAD = 64
MLA_WIDTH = MLA_HEADS * V_HEAD
MLA_SCALE = (QK_NOPE + QK_ROPE) ** -0.5
ROPE_THETA = 10000.0
LRU_WIDTH = 512
LRU_BLOCKS = 8
LRU_BLOCK = LRU_WIDTH // LRU_BLOCKS
CONV_W = 4
LRU_C = 8.0
MEM_HEADS = 4
MEM_HEAD_DIM = D_MODEL // MEM_HEADS
D_FF = 2816
EPS = 1e-6

LANES = 128
SUBLANES = 8
HEAD_PAD = LANES
MLA_PAD = MLA_HEADS * HEAD_PAD
ZC_Q = 0
ZC_KV = ZC_Q + Q_LORA
ZC_KPE = ZC_KV + KV_LORA
ZC_KPE_ROT = ZC_KPE + HEAD_PAD
ZC_XBR = ZC_KPE_ROT + HEAD_PAD
ZC_GBR = ZC_XBR + LRU_WIDTH
Z_COLS = ZC_GBR + LRU_WIDTH
NEG_BIG = -1e30
VMEM_LIMIT_BYTES = 56 * 1024 * 1024


def _const_spec(shape):
    nd = len(shape)
    return pl.BlockSpec(shape, lambda *_: (0,) * nd, pipeline_mode=pl.Buffered(1))


def _rms(x, g):
    ms = jnp.mean(x * x, axis=-1, keepdims=True)
    return x * lax.rsqrt(ms + EPS) * g


def _dot(a, b):
    return jnp.dot(a, b, preferred_element_type=F32)


def _dot_nt(a, b):
    return lax.dot_general(a, b, (((1,), (1,)), ((), ())), preferred_element_type=F32)


def _swiglu_half(x, g_ref, w1_ref, w3_ref, w2_ref, ff_chunk):
    h = _rms(x, g_ref[...]).astype(BF16)
    acc = jnp.zeros(x.shape, F32)
    for c in range(D_FF // ff_chunk):
        lo = c * ff_chunk
        a = _dot(h, w1_ref[:, lo:lo + ff_chunk])
        b = _dot(h, w3_ref[:, lo:lo + ff_chunk])
        g = (a * jax.nn.sigmoid(a) * b).astype(BF16)
        acc = acc + _dot(g, w2_ref[lo:lo + ff_chunk, :])
    return x + 0.5 * acc


def _head_kernel(x_ref, g1_ref, w1_ref, w3_ref, w2_ref, gmix_ref, win_ref, gq_ref, wuq_ref,
                 gkv_ref, wukv_ref, freq_ref, sign_ref,
                 x1_ref, q_ref, k_ref, v_ref, ckv_ref, kpe_ref, xbr_ref, gbr_ref,
                 *, tm, seq_len, pos_base, ff_chunk):
    x1 = _swiglu_half(x_ref[...], g1_ref, w1_ref, w3_ref, w2_ref, ff_chunk)
    x1_ref[...] = x1

    h = _rms(x1, gmix_ref[...]).astype(BF16)
    z = _dot(h, win_ref[...])

    row = lax.broadcasted_iota(jnp.int32, (tm, HEAD_PAD), 0) + pl.program_id(0) * tm
    pos = (row & (seq_len - 1)) + pos_base
    ang = pos.astype(F32) * freq_ref[...]
    cos = jnp.cos(ang)
    sin = jnp.sin(ang) * sign_ref[...]

    cqn = _rms(z[:, ZC_Q:ZC_Q + Q_LORA], gq_ref[...]).astype(BF16)
    qz = _dot(cqn, wuq_ref[...])
    cos_q = cos * MLA_SCALE
    sin_q = sin * MLA_SCALE
    for hd in range(MLA_HEADS):
        lo = hd * HEAD_PAD
        qh = qz[:, lo:lo + HEAD_PAD] * cos_q + qz[:, MLA_PAD + lo:MLA_PAD + lo + HEAD_PAD] * sin_q
        q_ref[:, lo:lo + HEAD_PAD] = qh.astype(BF16)

    ckv = _rms(z[:, ZC_KV:ZC_KV + KV_LORA], gkv_ref[...])
    ckv_ref[...] = ckv
    kpe_blk = z[:, ZC_KPE:ZC_KPE + HEAD_PAD] * cos + z[:, ZC_KPE_ROT:ZC_KPE_ROT + HEAD_PAD] * sin
    kpe_ref[...] = kpe_blk[:, QK_NOPE:QK_NOPE + QK_ROPE]

    kvz = _dot(ckv.astype(BF16), wukv_ref[...])
    for hd in range(MLA_HEADS):
        lo = hd * HEAD_PAD
        k_ref[:, lo:lo + HEAD_PAD] = (kvz[:, lo:lo + HEAD_PAD] + kpe_blk).astype(BF16)
    v_ref[...] = kvz[:, MLA_PAD:].astype(BF16)

    xbr_ref[...] = z[:, ZC_XBR:ZC_XBR + LRU_WIDTH]
    gbr_ref[...] = z[:, ZC_GBR:ZC_GBR + LRU_WIDTH]


def _head_call(x, wts, *, seq_len, pos_base, tm, ff_chunk):
    n = x.shape[0]
    assert n % tm == 0 and seq_len & (seq_len - 1) == 0
    row = lambda w: pl.BlockSpec((tm, w), lambda i: (i, 0))
    consts = [wts['ffn1_norm'], wts['ffn1_w1'], wts['ffn1_w3'], wts['ffn1_w2'], wts['mix_norm'],
              wts['w_in_ext'], wts['q_norm'], wts['w_uq_ext'], wts['kv_norm'], wts['w_ukv_ext'],
              wts['rope_freq'], wts['rope_sign']]
    out_shapes = [
        jax.ShapeDtypeStruct((n, D_MODEL), F32),
        jax.ShapeDtypeStruct((n, MLA_PAD), BF16),
        jax.ShapeDtypeStruct((n, MLA_PAD), BF16),
        jax.ShapeDtypeStruct((n, MLA_PAD), BF16),
        jax.ShapeDtypeStruct((n, KV_LORA), F32),
        jax.ShapeDtypeStruct((n, QK_ROPE), F32),
        jax.ShapeDtypeStruct((n, LRU_WIDTH), F32),
        jax.ShapeDtypeStruct((n, LRU_WIDTH), F32),
    ]
    return pl.pallas_call(
        functools.partial(_head_kernel, tm=tm, seq_len=seq_len, pos_base=pos_base, ff_chunk=ff_chunk),
        grid=(n // tm,),
        in_specs=[row(D_MODEL)] + [_const_spec(c.shape) for c in consts],
        out_specs=[row(s.shape[1]) for s in out_shapes],
        out_shape=out_shapes,
        compiler_params=pltpu.CompilerParams(dimension_semantics=("arbitrary",),
                                             vmem_limit_bytes=VMEM_LIMIT_BYTES),
        name="head",
    )(x, *consts)


def _attn_finalize(o_heads, gattn_ref, out_ref):
    ss = o_heads[0] * o_heads[0]
    for o in o_heads[1:]:
        ss = ss + o * o
    ms = jnp.sum(ss, axis=-1, keepdims=True) * (1.0 / MLA_WIDTH)
    inv = lax.rsqrt(ms + EPS)
    for hd, o in enumerate(o_heads):
        lo = hd * HEAD_PAD
        out_ref[:, lo:lo + HEAD_PAD] = (o * inv * gattn_ref[:, lo:lo + HEAD_PAD]).astype(BF16)


def _mla_prompt_kernel(q_ref, k_ref, v_ref, gattn_ref, out_ref, m_scr, l_scr, acc_scr, *, tq):
    i = pl.program_id(0)
    j = pl.program_id(1)

    @pl.when(j == 0)
    def _():
        m_scr[...] = jnp.full(m_scr.shape, NEG_BIG, F32)
        l_scr[...] = jnp.zeros(l_scr.shape, F32)
        acc_scr[...] = jnp.zeros(acc_scr.shape, F32)

    @pl.when(j <= i)
    def _():
        rq = lax.broadcasted_iota(jnp.int32, (tq, tq), 0) // CHUNK
        ck = lax.broadcasted_iota(jnp.int32, (tq, tq), 1) // CHUNK
        visible = (ck <= rq) | (j < i)
        for hd in range(MLA_HEADS):
            lo = hd * HEAD_PAD
            s = _dot_nt(q_ref[:, lo:lo + HEAD_PAD], k_ref[:, lo:lo + HEAD_PAD])
            s = jnp.where(visible, s, NEG_BIG)
            m_prev = m_scr[hd][:, :1]
            m_new = jnp.maximum(m_prev, jnp.max(s, axis=-1, keepdims=True))
            alpha = jnp.exp(m_prev - m_new)
            p = jnp.exp(s - m_new)
            l_new = alpha * l_scr[hd][:, :1] + jnp.sum(p, axis=-1, keepdims=True)
            acc_scr[:, lo:lo + HEAD_PAD] = (alpha * acc_scr[:, lo:lo + HEAD_PAD]
                                            + _dot(p.astype(BF16), v_ref[:, lo:lo + HEAD_PAD]))
            m_scr[hd] = jnp.broadcast_to(m_new, (tq, LANES))
            l_scr[hd] = jnp.broadcast_to(l_new, (tq, LANES))

    @pl.when(j == i)
    def _():
        o_heads = []
        for hd in range(MLA_HEADS):
            lo = hd * HEAD_PAD
            o_heads.append(acc_scr[:, lo:lo + HEAD_PAD] / l_scr[hd][:, :1])
        _attn_finalize(o_heads, gattn_ref, out_ref)


def _mla_prompt_call(q, k, v, gattn, *, tq):
    n = q.shape[0]
    assert n % tq == 0 and tq % CHUNK == 0
    nb = n // tq
    return pl.pallas_call(
        functools.partial(_mla_prompt_kernel, tq=tq),
        grid=(nb, nb),
        in_specs=[pl.BlockSpec((tq, MLA_PAD), lambda i, j: (i, 0)),
                  pl.BlockSpec((tq, MLA_PAD), lambda i, j: (jnp.minimum(i, j), 0)),
                  pl.BlockSpec((tq, MLA_PAD), lambda i, j: (jnp.minimum(i, j), 0)),
                  _const_spec(gattn.shape)],
        out_specs=pl.BlockSpec((tq, MLA_PAD), lambda i, j: (i, 0)),
        out_shape=jax.ShapeDtypeStruct((n, MLA_PAD), BF16),
        scratch_shapes=[pltpu.VMEM((MLA_HEADS, tq, LANES), F32),
                        pltpu.VMEM((MLA_HEADS, tq, LANES), F32),
                        pltpu.VMEM((tq, MLA_PAD), F32)],
        compiler_params=pltpu.CompilerParams(dimension_semantics=("arbitrary", "arbitrary"),
                                             vmem_limit_bytes=VMEM_LIMIT_BYTES),
        name="mla_prompt",
    )(q, k, v, gattn)


def _mla_sample_kernel(q_ref, kn_ref, vn_ref, ckv_ref, kpe_ref, wukv_ref, gattn_ref, out_ref):
    cp = ckv_ref[0].astype(BF16)
    kpe = kpe_ref[0]
    o_heads = []
    for hd in range(MLA_HEADS):
        lo = hd * HEAD_PAD
        qh = q_ref[:, lo:lo + HEAD_PAD]
        kp = (_dot(cp, wukv_ref[:, lo:lo + HEAD_PAD]) + kpe).astype(BF16)
        vp = _dot(cp, wukv_ref[:, MLA_PAD + lo:MLA_PAD + lo + HEAD_PAD]).astype(BF16)
        s1 = _dot_nt(qh, kp)
        s2 = _dot_nt(qh, kn_ref[:, lo:lo + HEAD_PAD])
        m = jnp.maximum(jnp.max(s1, axis=-1, keepdims=True), jnp.max(s2, axis=-1, keepdims=True))
        p1 = jnp.exp(s1 - m)
        p2 = jnp.exp(s2 - m)
        l = jnp.sum(p1, axis=-1, keepdims=True) + jnp.sum(p2, axis=-1, keepdims=True)
        o = _dot(p1.astype(BF16), vp) + _dot(p2.astype(BF16), vn_ref[:, lo:lo + HEAD_PAD])
        o_heads.append(o / l)
    _attn_finalize(o_heads, gattn_ref, out_ref)


def _mla_sample_call(q, k, v, ckv_past, kpe_past_pad, w_ukv_ext, gattn, *, t_new):
    n = q.shape[0]
    nbatch, past, _ = ckv_past.shape
    assert n == nbatch * t_new
    rows = pl.BlockSpec((t_new, MLA_PAD), lambda b: (b, 0))
    return pl.pallas_call(
        _mla_sample_kernel,
        grid=(nbatch,),
        in_specs=[rows, rows, rows,
                  pl.BlockSpec((1, past, KV_LORA), lambda b: (b, 0, 0)),
                  pl.BlockSpec((1, past, HEAD_PAD), lambda b: (b, 0, 0)),
                  _const_spec(w_ukv_ext.shape), _const_spec(gattn.shape)],
        out_specs=rows,
        out_shape=jax.ShapeDtypeStruct((n, MLA_PAD), BF16),
        compiler_params=pltpu.CompilerParams(dimension_semantics=("arbitrary",),
                                             vmem_limit_bytes=VMEM_LIMIT_BYTES),
        name="mla_sample",
    )(q, k, v, ckv_past, kpe_past_pad, w_ukv_ext, gattn)


def _lru_kernel(xbr_ref, gbr_ref, cinit_ref, hinit_ref, convw_ref, convb_ref, wg_ref, bg_ref,
                lam_ref, gl_ref, out_ref, cstate_ref, hstate_ref, xe_scr, hcar_scr, *, tm):
    t = pl.program_id(1)

    @pl.when(t == 0)
    def _():
        xe_scr[0:SUBLANES, :] = cinit_ref[0]
        hcar_scr[...] = hinit_ref[0]

    x = xbr_ref[...]
    xe_scr[SUBLANES:SUBLANES + tm, :] = x
    xc = convb_ref[...] + xe_scr[pl.ds(SUBLANES - 3, tm), :] * convw_ref[0:1, :]
    xc = xc + xe_scr[pl.ds(SUBLANES - 2, tm), :] * convw_ref[1:2, :]
    xc = xc + xe_scr[pl.ds(SUBLANES - 1, tm), :] * convw_ref[2:3, :]
    xc = xc + x * convw_ref[3:4, :]
    tail = x[tm - SUBLANES:, :]
    xe_scr[0:SUBLANES, :] = tail
    cstate_ref[0] = tail

    y = _dot(xc.astype(BF16), wg_ref[...]) + bg_ref[...]
    r = jax.nn.sigmoid(y[:, :LRU_WIDTH])
    ig = jax.nn.sigmoid(y[:, LRU_WIDTH:])
    nl = -lam_ref[...]
    softplus = jnp.maximum(nl, 0.0) + jnp.log1p(jnp.exp(-jnp.abs(nl)))
    log_a = -LRU_C * r * softplus
    a = jnp.exp(log_a)
    th = jnp.tanh(log_a)
    one_minus_a2 = -2.0 * th / (1.0 - th)
    b = jnp.sqrt(one_minus_a2) * ig * xc

    rowi = lax.broadcasted_iota(jnp.int32, (tm, LRU_WIDTH), 0)
    s = 1
    while s < tm:
        valid = rowi >= s
        a_sh = pltpu.roll(a, s, axis=0)
        b_sh = pltpu.roll(b, s, axis=0)
        b = jnp.where(valid, a * b_sh + b, b)
        a = jnp.where(valid, a * a_sh, a)
        s *= 2
    h = b + a * hcar_scr[0:1, :]
    hlast = jnp.broadcast_to(h[tm - 1:tm, :], (SUBLANES, LRU_WIDTH))
    hcar_scr[...] = hlast
    hstate_ref[0] = hlast

    out_ref[...] = _rms(jax.nn.gelu(gbr_ref[...]) * h, gl_ref[...]).astype(BF16)


def _lru_call(xbr, gbr, conv_init, h_init, wts, *, seq_len, tm):
    n = xbr.shape[0]
    nseq = n // seq_len
    assert seq_len % tm == 0 and tm >= SUBLANES and tm % SUBLANES == 0
    nt = seq_len // tm
    row = pl.BlockSpec((tm, LRU_WIDTH), lambda sq, t: (sq * nt + t, 0))
    state = pl.BlockSpec((1, SUBLANES, LRU_WIDTH), lambda sq, t: (sq, 0, 0))
    consts = [wts['conv_w'], wts['conv_b'], wts['lru_wg'], wts['lru_bg'], wts['lru_lambda'],
              wts['lru_out_norm']]
    return pl.pallas_call(
        functools.partial(_lru_kernel, tm=tm),
        grid=(nseq, nt),
        in_specs=[row, row, state, state] + [_const_spec(c.shape) for c in consts],
        out_specs=[row, state, state],
        out_shape=[jax.ShapeDtypeStruct((n, LRU_WIDTH), BF16),
                   jax.ShapeDtypeStruct((nseq, SUBLANES, LRU_WIDTH), F32),
                   jax.ShapeDtypeStruct((nseq, SUBLANES, LRU_WIDTH), F32)],
        scratch_shapes=[pltpu.VMEM((tm + SUBLANES, LRU_WIDTH), F32),
                        pltpu.VMEM((SUBLANES, LRU_WIDTH), F32)],
        compiler_params=pltpu.CompilerParams(dimension_semantics=("arbitrary", "arbitrary"),
                                             vmem_limit_bytes=VMEM_LIMIT_BYTES),
        name="lru",
    )(xbr, gbr, conv_init, h_init, *consts)


def _tail_kernel(x1_ref, attn_ref, lru_ref, wout_ref, gx_ref, wmq_ref, mk_ref, mv_ref, wmo_ref,
                 g2_ref, w1_ref, w3_ref, w2_ref, gf_ref, y_ref, *, tm, nb, ff_chunk):
    mix = _dot(attn_ref[...], wout_ref[0:MLA_PAD, :]) + _dot(lru_ref[...], wout_ref[MLA_PAD:, :])
    x2 = x1_ref[...] + mix

    hq = _rms(x2, gx_ref[...]).astype(BF16)
    q = (_dot(hq, wmq_ref[...]) * (MEM_HEAD_DIM ** -0.5)).astype(BF16)
    rows = tm // nb
    o_rows = []
    for b in range(nb):
        o_heads = []
        for hd in range(MEM_HEADS):
            lo = hd * MEM_HEAD_DIM
            s = _dot_nt(q[b * rows:(b + 1) * rows, lo:lo + MEM_HEAD_DIM], mk_ref[b, :, lo:lo + MEM_HEAD_DIM])
            p = jnp.exp(s - jnp.max(s, axis=-1, keepdims=True))
            l = jnp.sum(p, axis=-1, keepdims=True)
            o_heads.append(_dot(p.astype(BF16), mv_ref[b, :, lo:lo + MEM_HEAD_DIM]) / l)
        o_rows.append(jnp.concatenate(o_heads, axis=1))
    o = o_rows[0] if nb == 1 else jnp.concatenate(o_rows, axis=0)
    x3 = x2 + _dot(o.astype(BF16), wmo_ref[...])

    x4 = _swiglu_half(x3, g2_ref, w1_ref, w3_ref, w2_ref, ff_chunk)
    y_ref[...] = _rms(x4, gf_ref[...])


def _tail_call(x1, attn, lru, mk, mv, wts, *, seq_len, tm, nb, ff_chunk):
    n = x1.shape[0]
    assert n % tm == 0 and tm % nb == 0 and mk.shape[0] * seq_len == n
    assert (nb == 1 and seq_len % tm == 0) or nb * seq_len == tm
    row = lambda w: pl.BlockSpec((tm, w), lambda i: (i, 0))
    mem = pl.BlockSpec((nb, N_MEM, D_MODEL), lambda i: ((i * tm) // (seq_len * nb), 0, 0))
    c = lambda name: _const_spec(wts[name].shape)
    return pl.pallas_call(
        functools.partial(_tail_kernel, tm=tm, nb=nb, ff_chunk=ff_chunk),
        grid=(n // tm,),
        in_specs=[row(D_MODEL), row(MLA_PAD), row(LRU_WIDTH), c('w_out_ext'), c('xattn_norm'), c('w_mq'),
                  mem, mem, c('w_mo'), c('ffn2_norm'), c('ffn2_w1'), c('ffn2_w3'), c('ffn2_w2'),
                  c('final_norm')],
        out_specs=row(D_MODEL),
        out_shape=jax.ShapeDtypeStruct((n, D_MODEL), F32),
        compiler_params=pltpu.CompilerParams(dimension_semantics=("arbitrary",),
                                             vmem_limit_bytes=VMEM_LIMIT_BYTES),
        name="tail",
    )(x1, attn, lru, wts['w_out_ext'], wts['xattn_norm'], wts['w_mq'], mk, mv, wts['w_mo'],
      wts['ffn2_norm'], wts['ffn2_w1'], wts['ffn2_w3'], wts['ffn2_w2'], wts['final_norm'])


def _memkv_kernel(mem_ref, g_ref, wmk_ref, wmv_ref, k_ref, v_ref, kb_ref, vb_ref):
    m = _rms(mem_ref[...], g_ref[...]).astype(BF16)
    k = _dot(m, wmk_ref[...])
    v = _dot(m, wmv_ref[...])
    k_ref[...] = k
    v_ref[...] = v
    kb_ref[...] = k.astype(BF16)
    vb_ref[...] = v.astype(BF16)


def _memkv_call(mem, wts):
    n = mem.shape[0]
    full = lambda dt: jax.ShapeDtypeStruct((n, D_MODEL), dt)
    spec = pl.BlockSpec((n, D_MODEL), lambda i: (0, 0))
    return pl.pallas_call(
        _memkv_kernel,
        grid=(1,),
        in_specs=[spec, _const_spec(wts['mem_norm'].shape), _const_spec(wts['w_mk'].shape),
                  _const_spec(wts['w_mv'].shape)],
        out_specs=[spec, spec, spec, spec],
        out_shape=[full(F32), full(F32), full(BF16), full(BF16)],
        compiler_params=pltpu.CompilerParams(dimension_semantics=("arbitrary",),
                                             vmem_limit_bytes=VMEM_LIMIT_BYTES),
        name="memkv",
    )(mem, wts['mem_norm'], wts['w_mk'], wts['w_mv'])


def _prep_weights(p):
    w = {}
    rowv = lambda a: a.reshape(1, -1).astype(F32)
    for name in ('ffn1_norm', 'mix_norm', 'q_norm', 'kv_norm', 'xattn_norm', 'ffn2_norm', 'final_norm',
                 'mem_norm', 'lru_out_norm', 'conv_b', 'lru_lambda'):
        w[name] = rowv(p[name])
    for name in ('ffn1_w1', 'ffn1_w3', 'ffn1_w2', 'ffn2_w1', 'ffn2_w3', 'ffn2_w2', 'w_mq', 'w_mk', 'w_mv',
                 'w_mo'):
        w[name] = p[name].astype(BF16)

    half = QK_ROPE // 2
    w_in = p['w_in']
    kpe_cols = w_in[:, ZC_KPE:ZC_KPE + QK_ROPE]
    pad_l = jnp.zeros((D_MODEL, QK_NOPE), F32)
    pad_r = jnp.zeros((D_MODEL, HEAD_PAD - QK_NOPE - QK_ROPE), F32)
    kpe_blk = jnp.concatenate([pad_l, kpe_cols, pad_r], axis=1)
    kpe_rot = jnp.concatenate([pad_l, kpe_cols[:, half:], kpe_cols[:, :half], pad_r], axis=1)
    lru_cols = w_in[:, Q_LORA + KV_LORA + QK_ROPE:]
    w['w_in_ext'] = jnp.concatenate([w_in[:, :Q_LORA + KV_LORA], kpe_blk, kpe_rot, lru_cols], axis=1).astype(BF16)

    wq = p['w_uq'].reshape(Q_LORA, MLA_HEADS, QK_NOPE + QK_ROPE)
    zq = lambda n: jnp.zeros((Q_LORA, MLA_HEADS, n), F32)
    q_main = jnp.concatenate([wq, zq(HEAD_PAD - QK_NOPE - QK_ROPE)], axis=2)
    q_rot = jnp.concatenate([zq(QK_NOPE), wq[:, :, QK_NOPE + half:], wq[:, :, QK_NOPE:QK_NOPE + half],
                             zq(HEAD_PAD - QK_NOPE - QK_ROPE)], axis=2)
    w['w_uq_ext'] = jnp.concatenate([q_main.reshape(Q_LORA, MLA_PAD), q_rot.reshape(Q_LORA, MLA_PAD)],
                                    axis=1).astype(BF16)

    wkv = p['w_ukv'].reshape(KV_LORA, MLA_HEADS, QK_NOPE + V_HEAD)
    zk = jnp.zeros((KV_LORA, MLA_HEADS, HEAD_PAD - QK_NOPE), F32)
    zv = jnp.zeros((KV_LORA, MLA_HEADS, HEAD_PAD - V_HEAD), F32)
    k_part = jnp.concatenate([wkv[:, :, :QK_NOPE], zk], axis=2).reshape(KV_LORA, MLA_PAD)
    v_part = jnp.concatenate([wkv[:, :, QK_NOPE:], zv], axis=2).reshape(KV_LORA, MLA_PAD)
    w['w_ukv_ext'] = jnp.concatenate([k_part, v_part], axis=1).astype(BF16)

    inv_freq = ROPE_THETA ** (-jnp.arange(0, QK_ROPE, 2, dtype=F32) / QK_ROPE)
    zl = jnp.zeros((QK_NOPE,), F32)
    zr = jnp.zeros((HEAD_PAD - QK_NOPE - QK_ROPE,), F32)
    w['rope_freq'] = jnp.concatenate([zl, inv_freq, inv_freq, zr]).reshape(1, HEAD_PAD)
    w['rope_sign'] = jnp.concatenate([zl, -jnp.ones((half,), F32), jnp.ones((half,), F32), zr]).reshape(1, HEAD_PAD)

    g_attn = jnp.concatenate([p['attn_out_norm'].reshape(MLA_HEADS, V_HEAD),
                              jnp.zeros((MLA_HEADS, HEAD_PAD - V_HEAD), F32)], axis=1)
    w['attn_out_norm_pad'] = g_attn.reshape(1, MLA_PAD)

    w_out = p['w_out']
    wo_attn = jnp.concatenate([w_out[:MLA_WIDTH].reshape(MLA_HEADS, V_HEAD, D_MODEL),
                               jnp.zeros((MLA_HEADS, HEAD_PAD - V_HEAD, D_MODEL), F32)], axis=1)
    w['w_out_ext'] = jnp.concatenate([wo_attn.reshape(MLA_PAD, D_MODEL), w_out[MLA_WIDTH:]], axis=0).astype(BF16)

    def block_diag(wb):
        eye = jnp.eye(LRU_BLOCKS, dtype=F32)
        return (eye[:, None, :, None] * wb[:, :, None, :]).reshape(LRU_WIDTH, LRU_WIDTH)

    w['lru_wg'] = jnp.concatenate([block_diag(p['lru_wa']), block_diag(p['lru_wx'])], axis=1).astype(BF16)
    w['lru_bg'] = jnp.concatenate([p['lru_ba'].reshape(1, -1), p['lru_bx'].reshape(1, -1)], axis=1).astype(F32)
    w['conv_w'] = p['conv_w'].astype(F32)
    return w


def _pick_tile(n, cap):
    t = min(n, cap)
    while n % t:
        t //= 2
    return t


def _group(x, mk, mv, wts, *, past, conv_init, h_init):
    nbatch, seq, _ = x.shape
    n = nbatch * seq
    xf = x.reshape(n, D_MODEL)
    tm = _pick_tile(n, 256)
    ff_chunk = D_FF // 2
    pos_base = 0 if past is None else past[0].shape[1]
    x1, q, k, v, ckv, kpe, xbr, gbr = _head_call(xf, wts, seq_len=seq, pos_base=pos_base, tm=tm,
                                                  ff_chunk=ff_chunk)
    if past is None:
        assert nbatch == 1
        attn = _mla_prompt_call(q, k, v, wts['attn_out_norm_pad'], tq=_pick_tile(n, 512))
    else:
        ckv_past, kpe_past = past
        kpe_pad = jnp.pad(kpe_past, ((0, 0), (0, 0), (QK_NOPE, HEAD_PAD - QK_NOPE - QK_ROPE)))
        attn = _mla_sample_call(q, k, v, ckv_past, kpe_pad, wts['w_ukv_ext'], wts['attn_out_norm_pad'],
                                t_new=seq)
    lru, cstate, hstate = _lru_call(xbr, gbr, conv_init, h_init, wts, seq_len=seq, tm=_pick_tile(seq, 256))
    nb = tm // seq if seq < tm else 1
    y = _tail_call(x1, attn, lru, mk, mv, wts, seq_len=seq, tm=tm, nb=nb, ff_chunk=ff_chunk)
    return (y.reshape(nbatch, seq, D_MODEL), ckv.reshape(nbatch, seq, KV_LORA),
            kpe.reshape(nbatch, seq, QK_ROPE), cstate[:, SUBLANES - (CONV_W - 1):, :], hstate[:, 0, :])


def kernel(x_prompt, x_sample, mem_prompt, cache_mla_ckv, cache_mla_kpe, state_conv, state_lru, cache_mem_k, cache_mem_v, ffn1_norm, ffn1_w1, ffn1_w3, ffn1_w2, mix_norm, w_in, q_norm, w_uq, kv_norm, w_ukv, conv_w, conv_b, lru_wa, lru_ba, lru_wx, lru_bx, lru_lambda, attn_out_norm, lru_out_norm, w_out, mem_norm, xattn_norm, w_mq, w_mk, w_mv, w_mo, ffn2_norm, ffn2_w1, ffn2_w3, ffn2_w2, final_norm):
    depth = ffn1_norm.shape[0]
    assert depth == 1
    params = dict(ffn1_norm=ffn1_norm[0], ffn1_w1=ffn1_w1[0], ffn1_w3=ffn1_w3[0], ffn1_w2=ffn1_w2[0],
                  mix_norm=mix_norm[0], w_in=w_in[0], q_norm=q_norm[0], w_uq=w_uq[0], kv_norm=kv_norm[0],
                  w_ukv=w_ukv[0], conv_w=conv_w[0], conv_b=conv_b[0], lru_wa=lru_wa[0], lru_ba=lru_ba[0],
                  lru_wx=lru_wx[0], lru_bx=lru_bx[0], lru_lambda=lru_lambda[0],
                  attn_out_norm=attn_out_norm[0], lru_out_norm=lru_out_norm[0], w_out=w_out[0],
                  mem_norm=mem_norm[0], xattn_norm=xattn_norm[0], w_mq=w_mq[0], w_mk=w_mk[0], w_mv=w_mv[0],
                  w_mo=w_mo[0], ffn2_norm=ffn2_norm[0], ffn2_w1=ffn2_w1[0], ffn2_w3=ffn2_w3[0],
                  ffn2_w2=ffn2_w2[0], final_norm=final_norm)
    wts = _prep_weights(params)

    bp = x_prompt.shape[0]
    bs = x_sample.shape[0]
    n_mem = mem_prompt.shape[1]

    assert bp == 1
    mk_f, mv_f, mk_b, mv_b = _memkv_call(mem_prompt.reshape(bp * n_mem, D_MODEL), wts)
    zero_state = jnp.zeros((bp, SUBLANES, LRU_WIDTH), F32)
    y_p, ckv_p, kpe_p, conv_p, lru_p = _group(
        x_prompt, mk_b.reshape(bp, n_mem, D_MODEL), mv_b.reshape(bp, n_mem, D_MODEL), wts,
        past=None, conv_init=zero_state, h_init=zero_state)

    conv_init = jnp.pad(state_conv[0], ((0, 0), (SUBLANES - (CONV_W - 1), 0), (0, 0)))
    h_init = jnp.broadcast_to(state_lru[0][:, None, :], (bs, SUBLANES, LRU_WIDTH))
    y_s, ckv_s, kpe_s, conv_s, lru_s = _group(
        x_sample, cache_mem_k[0].reshape(bs, n_mem, D_MODEL).astype(BF16),
        cache_mem_v[0].reshape(bs, n_mem, D_MODEL).astype(BF16), wts,
        past=(cache_mla_ckv[0], cache_mla_kpe[0]), conv_init=conv_init, h_init=h_init)

    mem_shape = (1, bp, n_mem, MEM_HEADS, MEM_HEAD_DIM)
    return (y_p, y_s, ckv_p[None], kpe_p[None], conv_p[None], lru_p[None],
            mk_f.reshape(mem_shape), mv_f.reshape(mem_shape),
            ckv_s[None], kpe_s[None], conv_s[None], lru_s[None])
```

```python
import functools

import jax
import jax.numpy as jnp
from jax import lax
from jax.experimental import pallas as pl
from jax.experimental.pallas import tpu as pltpu

F32 = jnp.float32
BF16 = jnp.bfloat16

D_MODEL = 1024
CHUNK = 64
N_MEM = 256
MLA_HEADS = 8
Q_LORA = 384
KV_LORA = 256
QK_NOPE = 64
QK_ROPE = 32
V_HEAD = 64
MLA_WIDTH = MLA_HEADS * V_HEAD
MLA_SCALE = (QK_NOPE + QK_ROPE) ** -0.5
ROPE_THETA = 10000.0
LRU_WIDTH = 512
LRU_BLOCKS = 8
LRU_BLOCK = LRU_WIDTH // LRU_BLOCKS
CONV_W = 4
LRU_C = 8.0
MEM_HEADS = 4
MEM_HEAD_DIM = D_MODEL // MEM_HEADS
D_FF = 2816
EPS = 1e-6

LANES = 128
SUBLANES = 8
HEAD_PAD = LANES
MLA_PAD = MLA_HEADS * HEAD_PAD
VT_HEAD = V_HEAD + 16
VT_ROWS = MLA_HEADS * VT_HEAD
Q_SCALE = MLA_SCALE * 1.4426950408889634
QK_LOOKAHEAD = 2
ZC_Q = 0
ZC_KV = ZC_Q + Q_LORA
ZC_KPE = ZC_KV + KV_LORA
ZC_KPE_ROT = ZC_KPE + HEAD_PAD
ZC_XBR = ZC_KPE_ROT + HEAD_PAD
ZC_GBR = ZC_XBR + LRU_WIDTH
Z_COLS = ZC_GBR + LRU_WIDTH
NEG_BIG = -1e30
VMEM_LIMIT_BYTES = 56 * 1024 * 1024


def _const_spec(shape):
    nd = len(shape)
    return pl.BlockSpec(shape, lambda *_: (0,) * nd, pipeline_mode=pl.Buffered(1))


def _rms(x, g):
    ms = jnp.mean(x * x, axis=-1, keepdims=True)
    return x * lax.rsqrt(ms + EPS) * g


def _dot(a, b):
    return jnp.dot(a, b, preferred_element_type=F32)


def _dot_nt(a, b):
    return lax.dot_general(a, b, (((1,), (1,)), ((), ())), preferred_element_type=F32)


def _swiglu_half(x, g_ref, w1_ref, w3_ref, w2_ref, ff_chunk):
    h = _rms(x, g_ref[...]).astype(BF16)
    acc = jnp.zeros(x.shape, F32)
    for c in range(D_FF // ff_chunk):
        lo = c * ff_chunk
        a = _dot(h, w1_ref[:, lo:lo + ff_chunk])
        b = _dot(h, w3_ref[:, lo:lo + ff_chunk])
        g = (a * jax.nn.sigmoid(a) * b).astype(BF16)
        acc = acc + _dot(g, w2_ref[lo:lo + ff_chunk, :])
    return x + 0.5 * acc


def _head_kernel(x_ref, g1_ref, w1_ref, w3_ref, w2_ref, gmix_ref, win_ref, gq_ref, wuq_ref,
                 gkv_ref, wukv_ref, wuvt_ref, freq_ref, sign_ref,
                 x1_ref, q_ref, k_ref, v_ref, ckv_ref, kpe_ref, xbr_ref, gbr_ref,
                 *, tm, seq_len, pos_base, ff_chunk, v_transposed):
    x1 = _swiglu_half(x_ref[...], g1_ref, w1_ref, w3_ref, w2_ref, ff_chunk)
    x1_ref[...] = x1

    h = _rms(x1, gmix_ref[...]).astype(BF16)
    z = _dot(h, win_ref[...])

    row = lax.broadcasted_iota(jnp.int32, (tm, HEAD_PAD), 0) + pl.program_id(0) * tm
    pos = (row & (seq_len - 1)) + pos_base
    ang = pos.astype(F32) * freq_ref[...]
    cos = jnp.cos(ang)
    sin = jnp.sin(ang) * sign_ref[...]

    cqn = _rms(z[:, ZC_Q:ZC_Q + Q_LORA], gq_ref[...]).astype(BF16)
    qz = _dot(cqn, wuq_ref[...])
    cos_q = cos * Q_SCALE
    sin_q = sin * Q_SCALE
    for hd in range(MLA_HEADS):
        lo = hd * HEAD_PAD
        qh = qz[:, lo:lo + HEAD_PAD] * cos_q + qz[:, MLA_PAD + lo:MLA_PAD + lo + HEAD_PAD] * sin_q
        q_ref[:, lo:lo + HEAD_PAD] = qh.astype(BF16)

    ckv = _rms(z[:, ZC_KV:ZC_KV + KV_LORA], gkv_ref[...])
    ckv_ref[...] = ckv
    kpe_blk = z[:, ZC_KPE:ZC_KPE + HEAD_PAD] * cos + z[:, ZC_KPE_ROT:ZC_KPE_ROT + HEAD_PAD] * sin
    kpe_ref[...] = kpe_blk[:, QK_NOPE:QK_NOPE + QK_ROPE]

    ckv_b = ckv.astype(BF16)
    kz = _dot(ckv_b, wukv_ref[:, :MLA_PAD])
    for hd in range(MLA_HEADS):
        lo = hd * HEAD_PAD
        k_ref[:, lo:lo + HEAD_PAD] = (kz[:, lo:lo + HEAD_PAD] + kpe_blk).astype(BF16)
    if v_transposed:
        vt = _dot_nt(wuvt_ref[...], ckv_b)
        ridx = lax.broadcasted_iota(jnp.int32, vt.shape, 0)
        is_one = ridx == V_HEAD
        for hd in range(1, MLA_HEADS):
            is_one = is_one | (ridx == hd * VT_HEAD + V_HEAD)
        v_ref[...] = jnp.where(is_one, 1.0, vt).astype(BF16)
    else:
        v_ref[...] = _dot(ckv_b, wukv_ref[:, MLA_PAD:]).astype(BF16)

    xbr_ref[...] = z[:, ZC_XBR:ZC_XBR + LRU_WIDTH]
    gbr_ref[...] = z[:, ZC_GBR:ZC_GBR + LRU_WIDTH]


def _head_call(x, wts, *, seq_len, pos_base, tm, ff_chunk, v_transposed):
    n = x.shape[0]
    assert n % tm == 0 and seq_len & (seq_len - 1) == 0
    row = lambda w: pl.BlockSpec((tm, w), lambda i: (i, 0))
    consts = [wts['ffn1_norm'], wts['ffn1_w1'], wts['ffn1_w3'], wts['ffn1_w2'], wts['mix_norm'],
              wts['w_in_ext'], wts['q_norm'], wts['w_uq_ext'], wts['kv_norm'], wts['w_ukv_ext'],
              wts['w_uvt_ext'], wts['rope_freq'], wts['rope_sign']]
    v_shape = (VT_ROWS, n) if v_transposed else (n, MLA_PAD)
    v_spec = pl.BlockSpec((VT_ROWS, tm), lambda i: (0, i)) if v_transposed else row(MLA_PAD)
    out_shapes = [
        jax.ShapeDtypeStruct((n, D_MODEL), F32),
        jax.ShapeDtypeStruct((n, MLA_PAD), BF16),
        jax.ShapeDtypeStruct((n, MLA_PAD), BF16),
        jax.ShapeDtypeStruct(v_shape, BF16),
        jax.ShapeDtypeStruct((n, KV_LORA), F32),
        jax.ShapeDtypeStruct((n, QK_ROPE), F32),
        jax.ShapeDtypeStruct((n, LRU_WIDTH), F32),
        jax.ShapeDtypeStruct((n, LRU_WIDTH), F32),
    ]
    out_specs = [row(s.shape[1]) for s in out_shapes]
    out_specs[3] = v_spec
    return pl.pallas_call(
        functools.partial(_head_kernel, tm=tm, seq_len=seq_len, pos_base=pos_base, ff_chunk=ff_chunk,
                          v_transposed=v_transposed),
        grid=(n // tm,),
        in_specs=[row(D_MODEL)] + [_const_spec(c.shape) for c in consts],
        out_specs=out_specs,
        out_shape=out_shapes,
        compiler_params=pltpu.CompilerParams(dimension_semantics=("arbitrary",),
                                             vmem_limit_bytes=VMEM_LIMIT_BYTES),
        name="head",
    )(x, *consts)


def _attn_finalize(o_heads, gattn_ref, out_ref):
    ss = o_heads[0] * o_heads[0]
    for o in o_heads[1:]:
        ss = ss + o * o
    ms = jnp.sum(ss, axis=-1, keepdims=True) * (1.0 / MLA_WIDTH)
    inv = lax.rsqrt(ms + EPS)
    for hd, o in enumerate(o_heads):
        lo = hd * HEAD_PAD
        out_ref[:, lo:lo + HEAD_PAD] = (o * inv * gattn_ref[:, lo:lo + HEAD_PAD]).astype(BF16)


def _mla_prompt_step(q_ref, k_ref, vt_ref, m_scr, acc_scr, visible):
    def scores(hd):
        lo = hd * HEAD_PAD
        return _dot_nt(k_ref[:, lo:lo + HEAD_PAD], q_ref[:, lo:lo + HEAD_PAD])

    pending = [scores(hd) for hd in range(QK_LOOKAHEAD)]
    for hd in range(MLA_HEADS):
        st = pending[hd]
        if hd + QK_LOOKAHEAD < MLA_HEADS:
            pending.append(scores(hd + QK_LOOKAHEAD))
        if visible is not None:
            st = jnp.where(visible, st, NEG_BIG)
        m_prev = m_scr[hd:hd + 1, :]
        m_new = jnp.maximum(m_prev, jnp.max(st, axis=0, keepdims=True))
        alpha = jnp.exp2(m_prev - m_new)
        pt = jnp.exp2(st - m_new).astype(BF16)
        vo = hd * VT_HEAD
        acc_scr[vo:vo + VT_HEAD, :] = alpha * acc_scr[vo:vo + VT_HEAD, :] + _dot(vt_ref[vo:vo + VT_HEAD, :], pt)
        m_scr[hd:hd + 1, :] = m_new


def _mla_prompt_kernel(qi_ref, ki_ref, q_ref, k_ref, vt_ref, gattn_ref, out_ref, m_scr, acc_scr, *, tq, tk):
    step = pl.program_id(0)
    q_lo = qi_ref[step] * tq
    k_lo = ki_ref[step] * tk

    @pl.when(k_lo == 0)
    def _():
        m_scr[...] = jnp.full(m_scr.shape, NEG_BIG, F32)
        acc_scr[...] = jnp.zeros(acc_scr.shape, F32)

    fully_visible = k_lo + tk <= q_lo + CHUNK

    @pl.when(fully_visible)
    def _():
        _mla_prompt_step(q_ref, k_ref, vt_ref, m_scr, acc_scr, None)

    @pl.when(jnp.logical_not(fully_visible))
    def _():
        kc = (lax.broadcasted_iota(jnp.int32, (tk, tq), 0) + k_lo) // CHUNK
        qc = (lax.broadcasted_iota(jnp.int32, (tk, tq), 1) + q_lo) // CHUNK
        _mla_prompt_step(q_ref, k_ref, vt_ref, m_scr, acc_scr, kc <= qc)

    @pl.when(k_lo + tk >= q_lo + tq)
    def _():
        parts = []
        for hd in range(MLA_HEADS):
            vo = hd * VT_HEAD
            parts.append(acc_scr[vo:vo + V_HEAD, :] / acc_scr[vo + V_HEAD:vo + V_HEAD + 1, :])
        o = jnp.concatenate(parts, axis=0).T
        out_ref[...] = _rms(o, gattn_ref[...]).astype(BF16)


def _mla_prompt_call(q, k, vt, gattn, *, tq, tk):
    n = q.shape[0]
    assert n % tq == 0 and tq % tk == 0 and tk % CHUNK == 0
    pairs = [(qi, ki) for qi in range(n // tq) for ki in range((qi + 1) * tq // tk)]
    qi_tab = jnp.asarray([p[0] for p in pairs], jnp.int32)
    ki_tab = jnp.asarray([p[1] for p in pairs], jnp.int32)
    grid_spec = pltpu.PrefetchScalarGridSpec(
        num_scalar_prefetch=2,
        grid=(len(pairs),),
        in_specs=[pl.BlockSpec((tq, MLA_PAD), lambda s, qi, ki: (qi[s], 0)),
                  pl.BlockSpec((tk, MLA_PAD), lambda s, qi, ki: (ki[s], 0)),
                  pl.BlockSpec((VT_ROWS, tk), lambda s, qi, ki: (0, ki[s])),
                  pl.BlockSpec(gattn.shape, lambda s, qi, ki: (0, 0))],
        out_specs=pl.BlockSpec((tq, MLA_WIDTH), lambda s, qi, ki: (qi[s], 0)),
        scratch_shapes=[pltpu.VMEM((MLA_HEADS, tq), F32),
                        pltpu.VMEM((VT_ROWS, tq), F32)],
    )
    return pl.pallas_call(
        functools.partial(_mla_prompt_kernel, tq=tq, tk=tk),
        grid_spec=grid_spec,
        out_shape=jax.ShapeDtypeStruct((n, MLA_WIDTH), BF16),
        compiler_params=pltpu.CompilerParams(dimension_semantics=("arbitrary",),
                                             vmem_limit_bytes=VMEM_LIMIT_BYTES),
        name="mla_prompt",
    )(qi_tab, ki_tab, q, k, vt, gattn)


def _mla_sample_kernel(q_ref, kn_ref, vn_ref, ckv_ref, kpe_ref, wukv_ref, gattn_ref, out_ref):
    cp = ckv_ref[0].astype(BF16)
    kpe = kpe_ref[0]
    o_heads = []
    for hd in range(MLA_HEADS):
        lo = hd * HEAD_PAD
        qh = q_ref[:, lo:lo + HEAD_PAD]
        kp = (_dot(cp, wukv_ref[:, lo:lo + HEAD_PAD]) + kpe).astype(BF16)
        vp = _dot(cp, wukv_ref[:, MLA_PAD + lo:MLA_PAD + lo + HEAD_PAD]).astype(BF16)
        s1 = _dot_nt(qh, kp)
        s2 = _dot_nt(qh, kn_ref[:, lo:lo + HEAD_PAD])
        m = jnp.maximum(jnp.max(s1, axis=-1, keepdims=True), jnp.max(s2, axis=-1, keepdims=True))
        p1 = jnp.exp2(s1 - m)
        p2 = jnp.exp2(s2 - m)
        l = jnp.sum(p1, axis=-1, keepdims=True) + jnp.sum(p2, axis=-1, keepdims=True)
        o = _dot(p1.astype(BF16), vp) + _dot(p2.astype(BF16), vn_ref[:, lo:lo + HEAD_PAD])
        o_heads.append(o / l)
    _attn_finalize(o_heads, gattn_ref, out_ref)


def _mla_sample_call(q, k, v, ckv_past, kpe_past_pad, w_ukv_ext, gattn, *, t_new):
    n = q.shape[0]
    nbatch, past, _ = ckv_past.shape
    assert n == nbatch * t_new
    rows = pl.BlockSpec((t_new, MLA_PAD), lambda b: (b, 0))
    return pl.pallas_call(
        _mla_sample_kernel,
        grid=(nbatch,),
        in_specs=[rows, rows, rows,
                  pl.BlockSpec((1, past, KV_LORA), lambda b: (b, 0, 0)),
                  pl.BlockSpec((1, past, HEAD_PAD), lambda b: (b, 0, 0)),
                  _const_spec(w_ukv_ext.shape), _const_spec(gattn.shape)],
        out_specs=rows,
        out_shape=jax.ShapeDtypeStruct((n, MLA_PAD), BF16),
        compiler_params=pltpu.CompilerParams(dimension_semantics=("arbitrary",),
                                             vmem_limit_bytes=VMEM_LIMIT_BYTES),
        name="mla_sample",
    )(q, k, v, ckv_past, kpe_past_pad, w_ukv_ext, gattn)


def _lru_kernel(xbr_ref, gbr_ref, cinit_ref, hinit_ref, convw_ref, convb_ref, wg_ref, bg_ref,
                lam_ref, gl_ref, out_ref, cstate_ref, hstate_ref, xe_scr, hcar_scr, *, tm):
    t = pl.program_id(1)

    @pl.when(t == 0)
    def _():
        xe_scr[0:SUBLANES, :] = cinit_ref[0]
        hcar_scr[...] = hinit_ref[0]

    x = xbr_ref[...]
    xe_scr[SUBLANES:SUBLANES + tm, :] = x
    xc = convb_ref[...] + xe_scr[pl.ds(SUBLANES - 3, tm), :] * convw_ref[0:1, :]
    xc = xc + xe_scr[pl.ds(SUBLANES - 2, tm), :] * convw_ref[1:2, :]
    xc = xc + xe_scr[pl.ds(SUBLANES - 1, tm), :] * convw_ref[2:3, :]
    xc = xc + x * convw_ref[3:4, :]
    tail = x[tm - SUBLANES:, :]
    xe_scr[0:SUBLANES, :] = tail
    cstate_ref[0] = tail

    y = _dot(xc.astype(BF16), wg_ref[...]) + bg_ref[...]
    r = jax.nn.sigmoid(y[:, :LRU_WIDTH])
    ig = jax.nn.sigmoid(y[:, LRU_WIDTH:])
    nl = -lam_ref[...]
    softplus = jnp.maximum(nl, 0.0) + jnp.log1p(jnp.exp(-jnp.abs(nl)))
    log_a = -LRU_C * r * softplus
    a = jnp.exp(log_a)
    th = jnp.tanh(log_a)
    one_minus_a2 = -2.0 * th / (1.0 - th)
    b = jnp.sqrt(one_minus_a2) * ig * xc

    sub = lax.broadcasted_iota(jnp.int32, (tm, LRU_WIDTH), 0) & (SUBLANES - 1)
    s = 1
    while s < SUBLANES:
        valid = sub >= s
        a_sh = pltpu.roll(a, s, axis=0)
        b_sh = pltpu.roll(b, s, axis=0)
        b = jnp.where(valid, a * b_sh + b, b)
        a = jnp.where(valid, a * a_sh, a)
        s *= 2
    h_prev = hcar_scr[0:1, :]
    h_groups = []
    for g in range(tm // SUBLANES):
        lo = g * SUBLANES
        hg = b[lo:lo + SUBLANES, :] + a[lo:lo + SUBLANES, :] * h_prev
        h_groups.append(hg)
        h_prev = hg[SUBLANES - 1:SUBLANES, :]
    h = jnp.concatenate(h_groups, axis=0)
    hlast = jnp.broadcast_to(h_prev, (SUBLANES, LRU_WIDTH))
    hcar_scr[...] = hlast
    hstate_ref[0] = hlast

    out_ref[...] = _rms(jax.nn.gelu(gbr_ref[...]) * h, gl_ref[...]).astype(BF16)


def _lru_call(xbr, gbr, conv_init, h_init, wts, *, seq_len, tm):
    n = xbr.shape[0]
    nseq = n // seq_len
    assert seq_len % tm == 0 and tm >= SUBLANES and tm % SUBLANES == 0
    nt = seq_len // tm
    row = pl.BlockSpec((tm, LRU_WIDTH), lambda sq, t: (sq * nt + t, 0))
    state = pl.BlockSpec((1, SUBLANES, LRU_WIDTH), lambda sq, t: (sq, 0, 0))
    consts = [wts['conv_w'], wts['conv_b'], wts['lru_wg'], wts['lru_bg'], wts['lru_lambda'],
              wts['lru_out_norm']]
    return pl.pallas_call(
        functools.partial(_lru_kernel, tm=tm),
        grid=(nseq, nt),
        in_specs=[row, row, state, state] + [_const_spec(c.shape) for c in consts],
        out_specs=[row, state, state],
        out_shape=[jax.ShapeDtypeStruct((n, LRU_WIDTH), BF16),
                   jax.ShapeDtypeStruct((nseq, SUBLANES, LRU_WIDTH), F32),
                   jax.ShapeDtypeStruct((nseq, SUBLANES, LRU_WIDTH), F32)],
        scratch_shapes=[pltpu.VMEM((tm + SUBLANES, LRU_WIDTH), F32),
                        pltpu.VMEM((SUBLANES, LRU_WIDTH), F32)],
        compiler_params=pltpu.CompilerParams(dimension_semantics=("arbitrary", "arbitrary"),
                                             vmem_limit_bytes=VMEM_LIMIT_BYTES),
        name="lru",
    )(xbr, gbr, conv_init, h_init, *consts)


def _tail_kernel(x1_ref, attn_ref, lru_ref, woa_ref, wol_ref, gx_ref, wmq_ref, mk_ref, mv_ref, wmo_ref,
                 g2_ref, w1_ref, w3_ref, w2_ref, gf_ref, y_ref, *, tm, nb, ff_chunk):
    mix = _dot(attn_ref[...], woa_ref[...]) + _dot(lru_ref[...], wol_ref[...])
    x2 = x1_ref[...] + mix

    hq = _rms(x2, gx_ref[...]).astype(BF16)
    q = (_dot(hq, wmq_ref[...]) * (MEM_HEAD_DIM ** -0.5)).astype(BF16)
    rows = tm // nb
    o_rows = []
    for b in range(nb):
        o_heads = []
        for hd in range(MEM_HEADS):
            lo = hd * MEM_HEAD_DIM
            s = _dot_nt(q[b * rows:(b + 1) * rows, lo:lo + MEM_HEAD_DIM], mk_ref[b, :, lo:lo + MEM_HEAD_DIM])
            p = jnp.exp(s - jnp.max(s, axis=-1, keepdims=True))
            l = jnp.sum(p, axis=-1, keepdims=True)
            o_heads.append(_dot(p.astype(BF16), mv_ref[b, :, lo:lo + MEM_HEAD_DIM]) / l)
        o_rows.append(jnp.concatenate(o_heads, axis=1))
    o = o_rows[0] if nb == 1 else jnp.concatenate(o_rows, axis=0)
    x3 = x2 + _dot(o.astype(BF16), wmo_ref[...])

    x4 = _swiglu_half(x3, g2_ref, w1_ref, w3_ref, w2_ref, ff_chunk)
    y_ref[...] = _rms(x4, gf_ref[...])


def _tail_call(x1, attn, lru, mk, mv, wts, *, seq_len, tm, nb, ff_chunk):
    n = x1.shape[0]
    assert n % tm == 0 and tm % nb == 0 and mk.shape[0] * seq_len == n
    assert (nb == 1 and seq_len % tm == 0) or nb * seq_len == tm
    row = lambda w: pl.BlockSpec((tm, w), lambda i: (i, 0))
    mem = pl.BlockSpec((nb, N_MEM, D_MODEL), lambda i: ((i * tm) // (seq_len * nb), 0, 0))
    c = lambda name: _const_spec(wts[name].shape)
    w_out_attn = wts['w_out_attn'] if attn.shape[1] == MLA_WIDTH else wts['w_out_attn_pad']
    return pl.pallas_call(
        functools.partial(_tail_kernel, tm=tm, nb=nb, ff_chunk=ff_chunk),
        grid=(n // tm,),
        in_specs=[row(D_MODEL), row(attn.shape[1]), row(LRU_WIDTH), _const_spec(w_out_attn.shape),
                  c('w_out_lru'), c('xattn_norm'), c('w_mq'),
                  mem, mem, c('w_mo'), c('ffn2_norm'), c('ffn2_w1'), c('ffn2_w3'), c('ffn2_w2'),
                  c('final_norm')],
        out_specs=row(D_MODEL),
        out_shape=jax.ShapeDtypeStruct((n, D_MODEL), F32),
        compiler_params=pltpu.CompilerParams(dimension_semantics=("arbitrary",),
                                             vmem_limit_bytes=VMEM_LIMIT_BYTES),
        name="tail",
    )(x1, attn, lru, w_out_attn, wts['w_out_lru'], wts['xattn_norm'], wts['w_mq'], mk, mv, wts['w_mo'],
      wts['ffn2_norm'], wts['ffn2_w1'], wts['ffn2_w3'], wts['ffn2_w2'], wts['final_norm'])


def _memkv_kernel(mem_ref, g_ref, wmk_ref, wmv_ref, k_ref, v_ref, kb_ref, vb_ref):
    m = _rms(mem_ref[...], g_ref[...]).astype(BF16)
    k = _dot(m, wmk_ref[...])
    v = _dot(m, wmv_ref[...])
    k_ref[...] = k
    v_ref[...] = v
    kb_ref[...] = k.astype(BF16)
    vb_ref[...] = v.astype(BF16)


def _memkv_call(mem, wts):
    n = mem.shape[0]
    full = lambda dt: jax.ShapeDtypeStruct((n, D_MODEL), dt)
    spec = pl.BlockSpec((n, D_MODEL), lambda i: (0, 0))
    return pl.pallas_call(
        _memkv_kernel,
        grid=(1,),
        in_specs=[spec, _const_spec(wts['mem_norm'].shape), _const_spec(wts['w_mk'].shape),
                  _const_spec(wts['w_mv'].shape)],
        out_specs=[spec, spec, spec, spec],
        out_shape=[full(F32), full(F32), full(BF16), full(BF16)],
        compiler_params=pltpu.CompilerParams(dimension_semantics=("arbitrary",),
                                             vmem_limit_bytes=VMEM_LIMIT_BYTES),
        name="memkv",
    )(mem, wts['mem_norm'], wts['w_mk'], wts['w_mv'])


def _prep_weights(p):
    w = {}
    rowv = lambda a: a.reshape(1, -1).astype(F32)
    for name in ('ffn1_norm', 'mix_norm', 'q_norm', 'kv_norm', 'xattn_norm', 'ffn2_norm', 'final_norm',
                 'mem_norm', 'lru_out_norm', 'conv_b', 'lru_lambda'):
        w[name] = rowv(p[name])
    for name in ('ffn1_w1', 'ffn1_w3', 'ffn1_w2', 'ffn2_w1', 'ffn2_w3', 'ffn2_w2', 'w_mq', 'w_mk', 'w_mv',
                 'w_mo'):
        w[name] = p[name].astype(BF16)

    half = QK_ROPE // 2
    w_in = p['w_in']
    kpe_cols = w_in[:, ZC_KPE:ZC_KPE + QK_ROPE]
    pad_l = jnp.zeros((D_MODEL, QK_NOPE), F32)
    pad_r = jnp.zeros((D_MODEL, HEAD_PAD - QK_NOPE - QK_ROPE), F32)
    kpe_blk = jnp.concatenate([pad_l, kpe_cols, pad_r], axis=1)
    kpe_rot = jnp.concatenate([pad_l, kpe_cols[:, half:], kpe_cols[:, :half], pad_r], axis=1)
    lru_cols = w_in[:, Q_LORA + KV_LORA + QK_ROPE:]
    w['w_in_ext'] = jnp.concatenate([w_in[:, :Q_LORA + KV_LORA], kpe_blk, kpe_rot, lru_cols], axis=1).astype(BF16)

    wq = p['w_uq'].reshape(Q_LORA, MLA_HEADS, QK_NOPE + QK_ROPE)
    zq = lambda n: jnp.zeros((Q_LORA, MLA_HEADS, n), F32)
    q_main = jnp.concatenate([wq, zq(HEAD_PAD - QK_NOPE - QK_ROPE)], axis=2)
    q_rot = jnp.concatenate([zq(QK_NOPE), wq[:, :, QK_NOPE + half:], wq[:, :, QK_NOPE:QK_NOPE + half],
                             zq(HEAD_PAD - QK_NOPE - QK_ROPE)], axis=2)
    w['w_uq_ext'] = jnp.concatenate([q_main.reshape(Q_LORA, MLA_PAD), q_rot.reshape(Q_LORA, MLA_PAD)],
                                    axis=1).astype(BF16)

    wkv = p['w_ukv'].reshape(KV_LORA, MLA_HEADS, QK_NOPE + V_HEAD)
    zk = jnp.zeros((KV_LORA, MLA_HEADS, HEAD_PAD - QK_NOPE), F32)
    zv = jnp.zeros((KV_LORA, MLA_HEADS, HEAD_PAD - V_HEAD), F32)
    k_part = jnp.concatenate([wkv[:, :, :QK_NOPE], zk], axis=2).reshape(KV_LORA, MLA_PAD)
    v_part = jnp.concatenate([wkv[:, :, QK_NOPE:], zv], axis=2).reshape(KV_LORA, MLA_PAD)
    w['w_ukv_ext'] = jnp.concatenate([k_part, v_part], axis=1).astype(BF16)
    vt = jnp.concatenate([wkv[:, :, QK_NOPE:], jnp.zeros((KV_LORA, MLA_HEADS, VT_HEAD - V_HEAD), F32)], axis=2)
    w['w_uvt_ext'] = vt.reshape(KV_LORA, VT_ROWS).T.astype(BF16)

    inv_freq = ROPE_THETA ** (-jnp.arange(0, QK_ROPE, 2, dtype=F32) / QK_ROPE)
    zl = jnp.zeros((QK_NOPE,), F32)
    zr = jnp.zeros((HEAD_PAD - QK_NOPE - QK_ROPE,), F32)
    w['rope_freq'] = jnp.concatenate([zl, inv_freq, inv_freq, zr]).reshape(1, HEAD_PAD)
    w['rope_sign'] = jnp.concatenate([zl, -jnp.ones((half,), F32), jnp.ones((half,), F32), zr]).reshape(1, HEAD_PAD)

    g_attn = jnp.concatenate([p['attn_out_norm'].reshape(MLA_HEADS, V_HEAD),
                              jnp.zeros((MLA_HEADS, HEAD_PAD - V_HEAD), F32)], axis=1)
    w['attn_out_norm_pad'] = g_attn.reshape(1, MLA_PAD)
    w['attn_out_norm'] = rowv(p['attn_out_norm'])

    w_out = p['w_out']
    wo_attn = jnp.concatenate([w_out[:MLA_WIDTH].reshape(MLA_HEADS, V_HEAD, D_MODEL),
                               jnp.zeros((MLA_HEADS, HEAD_PAD - V_HEAD, D_MODEL), F32)], axis=1)
    w['w_out_attn_pad'] = wo_attn.reshape(MLA_PAD, D_MODEL).astype(BF16)
    w['w_out_attn'] = w_out[:MLA_WIDTH].astype(BF16)
    w['w_out_lru'] = w_out[MLA_WIDTH:].astype(BF16)

    def block_diag(wb):
        eye = jnp.eye(LRU_BLOCKS, dtype=F32)
        return (eye[:, None, :, None] * wb[:, :, None, :]).reshape(LRU_WIDTH, LRU_WIDTH)

    w['lru_wg'] = jnp.concatenate([block_diag(p['lru_wa']), block_diag(p['lru_wx'])], axis=1).astype(BF16)
    w['lru_bg'] = jnp.concatenate([p['lru_ba'].reshape(1, -1), p['lru_bx'].reshape(1, -1)], axis=1).astype(F32)
    w['conv_w'] = p['conv_w'].astype(F32)
    return w


def _pick_tile(n, cap):
    t = min(n, cap)
    while n % t:
        t //= 2
    return t


def _group(x, mk, mv, wts, *, past, conv_init, h_init):
    nbatch, seq, _ = x.shape
    n = nbatch * seq
    xf = x.reshape(n, D_MODEL)
    tm = _pick_tile(n, 512)
    ff_chunk = D_FF // 2
    pos_base = 0 if past is None else past[0].shape[1]
    x1, q, k, v, ckv, kpe, xbr, gbr = _head_call(xf, wts, seq_len=seq, pos_base=pos_base, tm=tm,
                                                  ff_chunk=ff_chunk, v_transposed=past is None)
    if past is None:
        assert nbatch == 1
        tq = _pick_tile(n, 1024)
        attn = _mla_prompt_call(q, k, v, wts['attn_out_norm'], tq=tq, tk=_pick_tile(tq, 512))
    else:
        ckv_past, kpe_past = past
        kpe_pad = jnp.pad(kpe_past, ((0, 0), (0, 0), (QK_NOPE, HEAD_PAD - QK_NOPE - QK_ROPE)))
        attn = _mla_sample_call(q, k, v, ckv_past, kpe_pad, wts['w_ukv_ext'], wts['attn_out_norm_pad'],
                                t_new=seq)
    lru, cstate, hstate = _lru_call(xbr, gbr, conv_init, h_init, wts, seq_len=seq, tm=_pick_tile(seq, 256))
    nb = tm // seq if seq < tm else 1
    y = _tail_call(x1, attn, lru, mk, mv, wts, seq_len=seq, tm=tm, nb=nb, ff_chunk=ff_chunk)
    return (y.reshape(nbatch, seq, D_MODEL), ckv.reshape(nbatch, seq, KV_LORA),
            kpe.reshape(nbatch, seq, QK_ROPE), cstate[:, SUBLANES - (CONV_W - 1):, :], hstate[:, 0, :])


def kernel(x_prompt, x_sample, mem_prompt, cache_mla_ckv, cache_mla_kpe, state_conv, state_lru, cache_mem_k, cache_mem_v, ffn1_norm, ffn1_w1, ffn1_w3, ffn1_w2, mix_norm, w_in, q_norm, w_uq, kv_norm, w_ukv, conv_w, conv_b, lru_wa, lru_ba, lru_wx, lru_bx, lru_lambda, attn_out_norm, lru_out_norm, w_out, mem_norm, xattn_norm, w_mq, w_mk, w_mv, w_mo, ffn2_norm, ffn2_w1, ffn2_w3, ffn2_w2, final_norm):
    depth = ffn1_norm.shape[0]
    assert depth == 1
    params = dict(ffn1_norm=ffn1_norm[0], ffn1_w1=ffn1_w1[0], ffn1_w3=ffn1_w3[0], ffn1_w2=ffn1_w2[0],
                  mix_norm=mix_norm[0], w_in=w_in[0], q_norm=q_norm[0], w_uq=w_uq[0], kv_norm=kv_norm[0],
                  w_ukv=w_ukv[0], conv_w=conv_w[0], conv_b=conv_b[0], lru_wa=lru_wa[0], lru_ba=lru_ba[0],
                  lru_wx=lru_wx[0], lru_bx=lru_bx[0], lru_lambda=lru_lambda[0],
                  attn_out_norm=attn_out_norm[0], lru_out_norm=lru_out_norm[0], w_out=w_out[0],
                  mem_norm=mem_norm[0], xattn_norm=xattn_norm[0], w_mq=w_mq[0], w_mk=w_mk[0], w_mv=w_mv[0],
                  w_mo=w_mo[0], ffn2_norm=ffn2_norm[0], ffn2_w1=ffn2_w1[0], ffn2_w3=ffn2_w3[0],
                  ffn2_w2=ffn2_w2[0], final_norm=final_norm)
    wts = _prep_weights(params)

    bp = x_prompt.shape[0]
    bs = x_sample.shape[0]
    n_mem = mem_prompt.shape[1]

    assert bp == 1
    mk_f, mv_f, mk_b, mv_b = _memkv_call(mem_prompt.reshape(bp * n_mem, D_MODEL), wts)
    zero_state = jnp.zeros((bp, SUBLANES, LRU_WIDTH), F32)
    y_p, ckv_p, kpe_p, conv_p, lru_p = _group(
        x_prompt, mk_b.reshape(bp, n_mem, D_MODEL), mv_b.reshape(bp, n_mem, D_MODEL), wts,
        past=None, conv_init=zero_state, h_init=zero_state)

    conv_init = jnp.pad(state_conv[0], ((0, 0), (SUBLANES - (CONV_W - 1), 0), (0, 0)))
    h_init = jnp.broadcast_to(state_lru[0][:, None, :], (bs, SUBLANES, LRU_WIDTH))
    y_s, ckv_s, kpe_s, conv_s, lru_s = _group(
        x_sample, cache_mem_k[0].reshape(bs, n_mem, D_MODEL).astype(BF16),
        cache_mem_v[0].reshape(bs, n_mem, D_MODEL).astype(BF16), wts,
        past=(cache_mla_ckv[0], cache_mla_kpe[0]), conv_init=conv_init, h_init=h_init)

    mem_shape = (1, bp, n_mem, MEM_HEADS, MEM_HEAD_DIM)
    return (y_p, y_s, ckv_p[None], kpe_p[None], conv_p[None], lru_p[None],
            mk_f.reshape(mem_shape), mv_f.reshape(mem_shape),
            ckv_s[None], kpe_s[None], conv_s[None], lru_s[None])
```

```python
import functools

import jax
import jax.numpy as jnp
from jax import lax
from jax.experimental import pallas as pl
from jax.experimental.pallas import tpu as pltpu

F32 = jnp.float32
BF16 = jnp.bfloat16

D_MODEL = 1024
CHUNK = 64
N_MEM = 256
MLA_HEADS = 8
Q_LORA = 384
KV_LORA = 256
QK_NOPE = 64
QK_ROPE = 32
V_HEAD = 64
MLA_WIDTH = MLA_HEADS * V_HEAD
MLA_SCALE = (QK_NOPE + QK_ROPE) ** -0.5
ROPE_THETA = 10000.0
LRU_WIDTH = 512
LRU_BLOCKS = 8
LRU_BLOCK = LRU_WIDTH // LRU_BLOCKS
CONV_W = 4
LRU_C = 8.0
MEM_HEADS = 4
MEM_HEAD_DIM = D_MODEL // MEM_HEADS
D_FF = 2816
EPS = 1e-6

LANES = 128
SUBLANES = 8
HEAD_PAD = LANES
MLA_PAD = MLA_HEADS * HEAD_PAD
VT_HEAD = V_HEAD + 16
VT_ROWS = MLA_HEADS * VT_HEAD
Q_SCALE = MLA_SCALE * 1.4426950408889634
QK_LOOKAHEAD = 2
RESCALE_THRESHOLD = 64.0
ZC_Q = 0
ZC_KV = ZC_Q + Q_LORA
ZC_KPE = ZC_KV + KV_LORA
ZC_KPE_ROT = ZC_KPE + HEAD_PAD
ZC_XBR = ZC_KPE_ROT + HEAD_PAD
ZC_GBR = ZC_XBR + LRU_WIDTH
Z_COLS = ZC_GBR + LRU_WIDTH
NEG_BIG = -1e30
VMEM_LIMIT_BYTES = 56 * 1024 * 1024


def _const_spec(shape):
    nd = len(shape)
    return pl.BlockSpec(shape, lambda *_: (0,) * nd, pipeline_mode=pl.Buffered(1))


def _rms(x, g):
    ms = jnp.mean(x * x, axis=-1, keepdims=True)
    return x * lax.rsqrt(ms + EPS) * g


def _dot(a, b):
    return jnp.dot(a, b, preferred_element_type=F32)


def _dot_nt(a, b):
    return lax.dot_general(a, b, (((1,), (1,)), ((), ())), preferred_element_type=F32)


def _swiglu_half(x, g_ref, w1_ref, w3_ref, w2_ref, ff_chunk):
    h = _rms(x, g_ref[...]).astype(BF16)
    acc = jnp.zeros(x.shape, F32)
    for c in range(D_FF // ff_chunk):
        lo = c * ff_chunk
        a = _dot(h, w1_ref[:, lo:lo + ff_chunk])
        b = _dot(h, w3_ref[:, lo:lo + ff_chunk])
        g = (a * jax.nn.sigmoid(a) * b).astype(BF16)
        acc = acc + _dot(g, w2_ref[lo:lo + ff_chunk, :])
    return x + 0.5 * acc


def _head_kernel(x_ref, cinit_ref, hinit_ref, g1_ref, w1_ref, w3_ref, w2_ref, gmix_ref, win_ref, gq_ref,
                 wuq_ref, gkv_ref, wukv_ref, wuvt_ref, freq_ref, sign_ref,
                 convw_ref, convb_ref, wg_ref, bg_ref, lam_ref, gl_ref,
                 x1_ref, q_ref, k_ref, v_ref, ckv_ref, kpe_ref, lru_ref, cstate_ref, hstate_ref,
                 xe_scr, ccar_scr, hcar_scr, zbuf_scr,
                 *, tm, seq_len, pos_base, ff_chunk, v_transposed, n_tiles, pipelined):
    step = pl.program_id(0)
    lru_consts = (convw_ref, convb_ref, wg_ref, bg_ref, lam_ref, gl_ref)
    bcast = lambda hrow: jnp.broadcast_to(hrow, (SUBLANES, LRU_WIDTH))
    if pipelined:
        @pl.when(step == 0)
        def _():
            zbuf_scr[...] = jnp.zeros(zbuf_scr.shape, F32)
            ccar_scr[...] = jnp.zeros(ccar_scr.shape, F32)
            hcar_scr[...] = jnp.zeros(hcar_scr.shape, F32)

        new_seq = lax.rem(step - 1, seq_len // tm) == 0
        conv_prev = jnp.where(new_seq, cinit_ref[0], ccar_scr[...])
        h_prev = jnp.where(new_seq, hinit_ref[0, 0:1, :], hcar_scr[0:1, :])
        out, tail, h_last = _lru_block(zbuf_scr[:, :LRU_WIDTH], zbuf_scr[:, LRU_WIDTH:], conv_prev, h_prev,
                                       xe_scr, *lru_consts)
        lru_ref[...] = out
        ccar_scr[...] = tail
        hcar_scr[...] = bcast(h_last)
        cstate_ref[0] = tail
        hstate_ref[0] = bcast(h_last)
        tile = jnp.minimum(step, n_tiles - 1)
    else:
        tile = step

    x1 = _swiglu_half(x_ref[...], g1_ref, w1_ref, w3_ref, w2_ref, ff_chunk)
    x1_ref[...] = x1

    h = _rms(x1, gmix_ref[...]).astype(BF16)
    z = _dot(h, win_ref[...])

    row = lax.broadcasted_iota(jnp.int32, (tm, HEAD_PAD), 0) + tile * tm
    pos = (row & (seq_len - 1)) + pos_base
    ang = pos.astype(F32) * freq_ref[...]
    cos = jnp.cos(ang)
    sin = jnp.sin(ang) * sign_ref[...]

    cqn = _rms(z[:, ZC_Q:ZC_Q + Q_LORA], gq_ref[...]).astype(BF16)
    qz = _dot(cqn, wuq_ref[...])
    cos_q = cos * Q_SCALE
    sin_q = sin * Q_SCALE
    for hd in range(MLA_HEADS):
        lo = hd * HEAD_PAD
        qh = qz[:, lo:lo + HEAD_PAD] * cos_q + qz[:, MLA_PAD + lo:MLA_PAD + lo + HEAD_PAD] * sin_q
        q_ref[:, lo:lo + HEAD_PAD] = qh.astype(BF16)

    ckv = _rms(z[:, ZC_KV:ZC_KV + KV_LORA], gkv_ref[...])
    ckv_ref[...] = ckv
    kpe_blk = z[:, ZC_KPE:ZC_KPE + HEAD_PAD] * cos + z[:, ZC_KPE_ROT:ZC_KPE_ROT + HEAD_PAD] * sin
    kpe_ref[...] = kpe_blk[:, QK_NOPE:QK_NOPE + QK_ROPE]

    ckv_b = ckv.astype(BF16)
    kz = _dot(ckv_b, wukv_ref[:, :MLA_PAD])
    for hd in range(MLA_HEADS):
        lo = hd * HEAD_PAD
        k_ref[:, lo:lo + HEAD_PAD] = (kz[:, lo:lo + HEAD_PAD] + kpe_blk).astype(BF16)
    if v_transposed:
        vt = _dot_nt(wuvt_ref[...], ckv_b)
        ridx = lax.broadcasted_iota(jnp.int32, vt.shape, 0)
        is_one = ridx == V_HEAD
        for hd in range(1, MLA_HEADS):
            is_one = is_one | (ridx == hd * VT_HEAD + V_HEAD)
        v_ref[...] = jnp.where(is_one, 1.0, vt).astype(BF16)
    else:
        v_ref[...] = _dot(ckv_b, wukv_ref[:, MLA_PAD:]).astype(BF16)

    xbr = z[:, ZC_XBR:ZC_XBR + LRU_WIDTH]
    gbr = z[:, ZC_GBR:ZC_GBR + LRU_WIDTH]
    if pipelined:
        zbuf_scr[:, :LRU_WIDTH] = xbr
        zbuf_scr[:, LRU_WIDTH:] = gbr
    elif seq_len >= tm:
        @pl.when(lax.rem(step, seq_len // tm) == 0)
        def _():
            ccar_scr[...] = cinit_ref[0]
            hcar_scr[...] = hinit_ref[0]

        out, tail, h_last = _lru_block(xbr, gbr, ccar_scr[...], hcar_scr[0:1, :], xe_scr, *lru_consts)
        ccar_scr[...] = tail
        hcar_scr[...] = bcast(h_last)
        lru_ref[...] = out
        cstate_ref[0] = tail
        hstate_ref[0] = bcast(h_last)
    else:
        for b in range(tm // seq_len):
            lo = b * seq_len
            out, tail, h_last = _lru_block(xbr[lo:lo + seq_len, :], gbr[lo:lo + seq_len, :], cinit_ref[b],
                                           hinit_ref[b, 0:1, :], xe_scr, *lru_consts)
            lru_ref[lo:lo + seq_len, :] = out
            cstate_ref[b] = tail
            hstate_ref[b] = bcast(h_last)


def _head_call(x, conv_init, h_init, wts, *, seq_len, pos_base, tm, ff_chunk, v_transposed):
    n = x.shape[0]
    nseq = n // seq_len
    assert n % tm == 0 and seq_len & (seq_len - 1) == 0 and seq_len >= SUBLANES
    assert seq_len % tm == 0 or tm % seq_len == 0
    nsub = max(1, tm // seq_len)
    n_tiles = n // tm
    pipelined = seq_len > tm
    if pipelined:
        row = lambda w: pl.BlockSpec((tm, w), lambda i: (jnp.minimum(i, n_tiles - 1), 0))
        lru_tile = lambda i: jnp.maximum(i - 1, 0)
        state = pl.BlockSpec((1, SUBLANES, LRU_WIDTH), lambda i: ((lru_tile(i) * tm) // seq_len, 0, 0))
    else:
        row = lambda w: pl.BlockSpec((tm, w), lambda i: (i, 0))
        state = pl.BlockSpec((nsub, SUBLANES, LRU_WIDTH), lambda i: ((i * tm) // (seq_len * nsub), 0, 0))
    consts = [wts['ffn1_norm'], wts['ffn1_w1'], wts['ffn1_w3'], wts['ffn1_w2'], wts['mix_norm'],
              wts['w_in_ext'], wts['q_norm'], wts['w_uq_ext'], wts['kv_norm'], wts['w_ukv_ext'],
              wts['w_uvt_ext'], wts['rope_freq'], wts['rope_sign'],
              wts['conv_w'], wts['conv_b'], wts['lru_wg'], wts['lru_bg'], wts['lru_lambda'],
              wts['lru_out_norm']]
    v_shape = (VT_ROWS, n) if v_transposed else (n, MLA_PAD)
    if v_transposed:
        v_spec = pl.BlockSpec((VT_ROWS, tm), lambda i: (0, jnp.minimum(i, n_tiles - 1)))
    else:
        v_spec = row(MLA_PAD)
    state_shape = jax.ShapeDtypeStruct((nseq, SUBLANES, LRU_WIDTH), F32)
    out_shapes = [
        jax.ShapeDtypeStruct((n, D_MODEL), F32),
        jax.ShapeDtypeStruct((n, MLA_PAD), BF16),
        jax.ShapeDtypeStruct((n, MLA_PAD), BF16),
        jax.ShapeDtypeStruct(v_shape, BF16),
        jax.ShapeDtypeStruct((n, KV_LORA), F32),
        jax.ShapeDtypeStruct((n, QK_ROPE), F32),
        jax.ShapeDtypeStruct((n, LRU_WIDTH), BF16),
        state_shape,
        state_shape,
    ]
    out_specs = [row(s.shape[1]) for s in out_shapes[:7]] + [state, state]
    out_specs[3] = v_spec
    if pipelined:
        out_specs[6] = pl.BlockSpec((tm, LRU_WIDTH), lambda i: (lru_tile(i), 0))
    lru_rows = min(tm, seq_len)
    return pl.pallas_call(
        functools.partial(_head_kernel, tm=tm, seq_len=seq_len, pos_base=pos_base, ff_chunk=ff_chunk,
                          v_transposed=v_transposed, n_tiles=n_tiles, pipelined=pipelined),
        grid=(n_tiles + 1 if pipelined else n_tiles,),
        in_specs=[row(D_MODEL), state, state] + [_const_spec(c.shape) for c in consts],
        out_specs=out_specs,
        out_shape=out_shapes,
        scratch_shapes=[pltpu.VMEM((lru_rows + SUBLANES, LRU_WIDTH), F32),
                        pltpu.VMEM((SUBLANES, LRU_WIDTH), F32),
                        pltpu.VMEM((SUBLANES, LRU_WIDTH), F32),
                        pltpu.VMEM((tm if pipelined else SUBLANES, 2 * LRU_WIDTH), F32)],
        compiler_params=pltpu.CompilerParams(dimension_semantics=("arbitrary",),
                                             vmem_limit_bytes=VMEM_LIMIT_BYTES),
        name="head",
    )(x, conv_init, h_init, *consts)


def _attn_finalize(o_heads, gattn_ref, out_ref):
    ss = o_heads[0] * o_heads[0]
    for o in o_heads[1:]:
        ss = ss + o * o
    ms = jnp.sum(ss, axis=-1, keepdims=True) * (1.0 / MLA_WIDTH)
    inv = lax.rsqrt(ms + EPS)
    for hd, o in enumerate(o_heads):
        lo = hd * HEAD_PAD
        out_ref[:, lo:lo + HEAD_PAD] = (o * inv * gattn_ref[:, lo:lo + HEAD_PAD]).astype(BF16)


def _mla_prompt_step(q_ref, k_ref, vt_ref, m_scr, mx_scr, acc_scr, cur, visible, *, rescale):
    def scores(hd):
        lo = hd * HEAD_PAD
        return _dot_nt(k_ref[:, lo:lo + HEAD_PAD], q_ref[:, lo:lo + HEAD_PAD])

    pending = [scores(hd) for hd in range(QK_LOOKAHEAD)]
    for hd in range(MLA_HEADS):
        st = pending[hd]
        if hd + QK_LOOKAHEAD < MLA_HEADS:
            pending.append(scores(hd + QK_LOOKAHEAD))
        if visible is not None:
            st = jnp.where(visible, st, NEG_BIG)
        vo = hd * VT_HEAD
        blk_max = jnp.max(st, axis=0, keepdims=True)
        if rescale:
            m_prev = m_scr[hd:hd + 1, :]
            m_new = jnp.maximum(m_prev, blk_max)
            alpha = jnp.exp2(m_prev - m_new)
            pt = jnp.exp2(st - m_new).astype(BF16)
            acc_scr[cur, vo:vo + VT_HEAD, :] = (alpha * acc_scr[cur, vo:vo + VT_HEAD, :]
                                                + _dot(vt_ref[vo:vo + VT_HEAD, :], pt))
            m_scr[hd:hd + 1, :] = m_new
        else:
            mx_scr[hd:hd + 1, :] = blk_max
            pt = jnp.exp2(st - m_scr[hd:hd + 1, :]).astype(BF16)
            acc_scr[1 - cur, vo:vo + VT_HEAD, :] = (acc_scr[cur, vo:vo + VT_HEAD, :]
                                                    + _dot(vt_ref[vo:vo + VT_HEAD, :], pt))


def _mla_prompt_kernel(qi_ref, ki_ref, q_ref, k_ref, vt_ref, gattn_ref, out_ref,
                       m_scr, mx_scr, acc_scr, cur_scr, *, tq, tk):
    step = pl.program_id(0)
    q_lo = qi_ref[step] * tq
    k_lo = ki_ref[step] * tk
    first = k_lo == 0

    @pl.when(first)
    def _():
        m_scr[...] = jnp.full(m_scr.shape, NEG_BIG, F32)
        mx_scr[...] = jnp.full(mx_scr.shape, NEG_BIG, F32)
        acc_scr[0] = jnp.zeros(acc_scr.shape[1:], F32)
        cur_scr[0] = 0

    cur = cur_scr[0]
    fully_visible = k_lo + tk <= q_lo + CHUNK

    def chunk_mask():
        kc = (lax.broadcasted_iota(jnp.int32, (tk, tq), 0) + k_lo) // CHUNK
        qc = (lax.broadcasted_iota(jnp.int32, (tk, tq), 1) + q_lo) // CHUNK
        return kc <= qc

    step_fn = functools.partial(_mla_prompt_step, q_ref, k_ref, vt_ref, m_scr, mx_scr, acc_scr, cur)

    @pl.when(jnp.logical_not(first) & fully_visible)
    def _():
        step_fn(None, rescale=False)

    @pl.when(jnp.logical_not(first) & jnp.logical_not(fully_visible))
    def _():
        step_fn(chunk_mask(), rescale=False)

    redo = first | (jnp.max(mx_scr[...] - m_scr[...]) > RESCALE_THRESHOLD)

    @pl.when(redo)
    def _():
        step_fn(chunk_mask(), rescale=True)

    @pl.when(jnp.logical_not(redo))
    def _():
        cur_scr[0] = 1 - cur

    @pl.when(k_lo + tk >= q_lo + tq)
    def _():
        fin = cur_scr[0]
        parts = []
        for hd in range(MLA_HEADS):
            vo = hd * VT_HEAD
            parts.append(acc_scr[fin, vo:vo + V_HEAD, :] / acc_scr[fin, vo + V_HEAD:vo + V_HEAD + 1, :])
        o = jnp.concatenate(parts, axis=0).T
        out_ref[...] = _rms(o, gattn_ref[...]).astype(BF16)


def _mla_prompt_call(q, k, vt, gattn, *, tq, tk):
    n = q.shape[0]
    assert n % tq == 0 and tq % tk == 0 and tk % CHUNK == 0
    pairs = [(qi, ki) for qi in range(n // tq) for ki in range((qi + 1) * tq // tk)]
    qi_tab = jnp.asarray([p[0] for p in pairs], jnp.int32)
    ki_tab = jnp.asarray([p[1] for p in pairs], jnp.int32)
    grid_spec = pltpu.PrefetchScalarGridSpec(
        num_scalar_prefetch=2,
        grid=(len(pairs),),
        in_specs=[pl.BlockSpec((tq, MLA_PAD), lambda s, qi, ki: (qi[s], 0)),
                  pl.BlockSpec((tk, MLA_PAD), lambda s, qi, ki: (ki[s], 0)),
                  pl.BlockSpec((VT_ROWS, tk), lambda s, qi, ki: (0, ki[s])),
                  pl.BlockSpec(gattn.shape, lambda s, qi, ki: (0, 0))],
        out_specs=pl.BlockSpec((tq, MLA_WIDTH), lambda s, qi, ki: (qi[s], 0)),
        scratch_shapes=[pltpu.VMEM((MLA_HEADS, tq), F32),
                        pltpu.VMEM((MLA_HEADS, tq), F32),
                        pltpu.VMEM((2, VT_ROWS, tq), F32),
                        pltpu.SMEM((1,), jnp.int32)],
    )
    return pl.pallas_call(
        functools.partial(_mla_prompt_kernel, tq=tq, tk=tk),
        grid_spec=grid_spec,
        out_shape=jax.ShapeDtypeStruct((n, MLA_WIDTH), BF16),
        compiler_params=pltpu.CompilerParams(dimension_semantics=("arbitrary",),
                                             vmem_limit_bytes=VMEM_LIMIT_BYTES),
        name="mla_prompt",
    )(qi_tab, ki_tab, q, k, vt, gattn)


def _mla_sample_kernel(q_ref, kn_ref, vn_ref, ckv_ref, kpe_ref, wukv_ref, gattn_ref, out_ref):
    cp = ckv_ref[0].astype(BF16)
    kpe = kpe_ref[0]
    o_heads = []
    for hd in range(MLA_HEADS):
        lo = hd * HEAD_PAD
        qh = q_ref[:, lo:lo + HEAD_PAD]
        kp = (_dot(cp, wukv_ref[:, lo:lo + HEAD_PAD]) + kpe).astype(BF16)
        vp = _dot(cp, wukv_ref[:, MLA_PAD + lo:MLA_PAD + lo + HEAD_PAD]).astype(BF16)
        s1 = _dot_nt(qh, kp)
        s2 = _dot_nt(qh, kn_ref[:, lo:lo + HEAD_PAD])
        m = jnp.maximum(jnp.max(s1, axis=-1, keepdims=True), jnp.max(s2, axis=-1, keepdims=True))
        p1 = jnp.exp2(s1 - m)
        p2 = jnp.exp2(s2 - m)
        l = jnp.sum(p1, axis=-1, keepdims=True) + jnp.sum(p2, axis=-1, keepdims=True)
        o = _dot(p1.astype(BF16), vp) + _dot(p2.astype(BF16), vn_ref[:, lo:lo + HEAD_PAD])
        o_heads.append(o / l)
    _attn_finalize(o_heads, gattn_ref, out_ref)


def _mla_sample_call(q, k, v, ckv_past, kpe_past_pad, w_ukv_ext, gattn, *, t_new):
    n = q.shape[0]
    nbatch, past, _ = ckv_past.shape
    assert n == nbatch * t_new
    rows = pl.BlockSpec((t_new, MLA_PAD), lambda b: (b, 0))
    return pl.pallas_call(
        _mla_sample_kernel,
        grid=(nbatch,),
        in_specs=[rows, rows, rows,
                  pl.BlockSpec((1, past, KV_LORA), lambda b: (b, 0, 0)),
                  pl.BlockSpec((1, past, HEAD_PAD), lambda b: (b, 0, 0)),
                  _const_spec(w_ukv_ext.shape), _const_spec(gattn.shape)],
        out_specs=rows,
        out_shape=jax.ShapeDtypeStruct((n, MLA_PAD), BF16),
        compiler_params=pltpu.CompilerParams(dimension_semantics=("arbitrary",),
                                             vmem_limit_bytes=VMEM_LIMIT_BYTES),
        name="mla_sample",
    )(q, k, v, ckv_past, kpe_past_pad, w_ukv_ext, gattn)


def _lru_block(x, gate, conv_prev, h_prev, xe_scr, convw_ref, convb_ref, wg_ref, bg_ref, lam_ref, gl_ref):
    tm = x.shape[0]
    xe_scr[0:SUBLANES, :] = conv_prev
    xe_scr[SUBLANES:SUBLANES + tm, :] = x
    xc = convb_ref[...] + xe_scr[pl.ds(SUBLANES - 3, tm), :] * convw_ref[0:1, :]
    xc = xc + xe_scr[pl.ds(SUBLANES - 2, tm), :] * convw_ref[1:2, :]
    xc = xc + xe_scr[pl.ds(SUBLANES - 1, tm), :] * convw_ref[2:3, :]
    xc = xc + x * convw_ref[3:4, :]
    tail = x[tm - SUBLANES:, :]

    y = _dot(xc.astype(BF16), wg_ref[...]) + bg_ref[...]
    r = jax.nn.sigmoid(y[:, :LRU_WIDTH])
    ig = jax.nn.sigmoid(y[:, LRU_WIDTH:])
    nl = -lam_ref[...]
    softplus = jnp.maximum(nl, 0.0) + jnp.log1p(jnp.exp(-jnp.abs(nl)))
    log_a = -LRU_C * r * softplus
    a = jnp.exp(log_a)
    th = jnp.tanh(log_a)
    one_minus_a2 = -2.0 * th / (1.0 - th)
    b = jnp.sqrt(one_minus_a2) * ig * xc

    sub = lax.broadcasted_iota(jnp.int32, (tm, LRU_WIDTH), 0) & (SUBLANES - 1)
    s = 1
    while s < SUBLANES:
        valid = sub >= s
        a_sh = pltpu.roll(a, s, axis=0)
        b_sh = pltpu.roll(b, s, axis=0)
        b = jnp.where(valid, a * b_sh + b, b)
        a = jnp.where(valid, a * a_sh, a)
        s *= 2
    h_groups = []
    for g in range(tm // SUBLANES):
        lo = g * SUBLANES
        hg = b[lo:lo + SUBLANES, :] + a[lo:lo + SUBLANES, :] * h_prev
        h_groups.append(hg)
        h_prev = hg[SUBLANES - 1:SUBLANES, :]
    h = jnp.concatenate(h_groups, axis=0)
    out = _rms(jax.nn.gelu(gate) * h, gl_ref[...]).astype(BF16)
    return out, tail, h_prev


def _tail_kernel(x1_ref, attn_ref, lru_ref, woa_ref, wol_ref, gx_ref, wmq_ref, mk_ref, mv_ref, wmo_ref,
                 g2_ref, w1_ref, w3_ref, w2_ref, gf_ref, y_ref, *, tm, nb, ff_chunk):
    mix = _dot(attn_ref[...], woa_ref[...]) + _dot(lru_ref[...], wol_ref[...])
    x2 = x1_ref[...] + mix

    hq = _rms(x2, gx_ref[...]).astype(BF16)
    q = (_dot(hq, wmq_ref[...]) * (MEM_HEAD_DIM ** -0.5)).astype(BF16)
    rows = tm // nb
    o_rows = []
    for b in range(nb):
        o_heads = []
        for hd in range(MEM_HEADS):
            lo = hd * MEM_HEAD_DIM
            s = _dot_nt(q[b * rows:(b + 1) * rows, lo:lo + MEM_HEAD_DIM], mk_ref[b, :, lo:lo + MEM_HEAD_DIM])
            p = jnp.exp(s - jnp.max(s, axis=-1, keepdims=True))
            l = jnp.sum(p, axis=-1, keepdims=True)
            o_heads.append(_dot(p.astype(BF16), mv_ref[b, :, lo:lo + MEM_HEAD_DIM]) / l)
        o_rows.append(jnp.concatenate(o_heads, axis=1))
    o = o_rows[0] if nb == 1 else jnp.concatenate(o_rows, axis=0)
    x3 = x2 + _dot(o.astype(BF16), wmo_ref[...])

    x4 = _swiglu_half(x3, g2_ref, w1_ref, w3_ref, w2_ref, ff_chunk)
    y_ref[...] = _rms(x4, gf_ref[...])


def _tail_call(x1, attn, lru, mk, mv, wts, *, seq_len, tm, nb, ff_chunk):
    n = x1.shape[0]
    assert n % tm == 0 and tm % nb == 0 and mk.shape[0] * seq_len == n
    assert (nb == 1 and seq_len % tm == 0) or nb * seq_len == tm
    row = lambda w: pl.BlockSpec((tm, w), lambda i: (i, 0))
    mem = pl.BlockSpec((nb, N_MEM, D_MODEL), lambda i: ((i * tm) // (seq_len * nb), 0, 0))
    c = lambda name: _const_spec(wts[name].shape)
    w_out_attn = wts['w_out_attn'] if attn.shape[1] == MLA_WIDTH else wts['w_out_attn_pad']
    return pl.pallas_call(
        functools.partial(_tail_kernel, tm=tm, nb=nb, ff_chunk=ff_chunk),
        grid=(n // tm,),
        in_specs=[row(D_MODEL), row(attn.shape[1]), row(LRU_WIDTH), _const_spec(w_out_attn.shape),
                  c('w_out_lru'), c('xattn_norm'), c('w_mq'),
                  mem, mem, c('w_mo'), c('ffn2_norm'), c('ffn2_w1'), c('ffn2_w3'), c('ffn2_w2'),
                  c('final_norm')],
        out_specs=row(D_MODEL),
        out_shape=jax.ShapeDtypeStruct((n, D_MODEL), F32),
        compiler_params=pltpu.CompilerParams(dimension_semantics=("arbitrary",),
                                             vmem_limit_bytes=VMEM_LIMIT_BYTES),
        name="tail",
    )(x1, attn, lru, w_out_attn, wts['w_out_lru'], wts['xattn_norm'], wts['w_mq'], mk, mv, wts['w_mo'],
      wts['ffn2_norm'], wts['ffn2_w1'], wts['ffn2_w3'], wts['ffn2_w2'], wts['final_norm'])


def _memkv_kernel(mem_ref, g_ref, wmk_ref, wmv_ref, k_ref, v_ref, kb_ref, vb_ref):
    m = _rms(mem_ref[...], g_ref[...]).astype(BF16)
    k = _dot(m, wmk_ref[...])
    v = _dot(m, wmv_ref[...])
    k_ref[...] = k
    v_ref[...] = v
    kb_ref[...] = k.astype(BF16)
    vb_ref[...] = v.astype(BF16)


def _memkv_call(mem, wts):
    n = mem.shape[0]
    full = lambda dt: jax.ShapeDtypeStruct((n, D_MODEL), dt)
    spec = pl.BlockSpec((n, D_MODEL), lambda i: (0, 0))
    return pl.pallas_call(
        _memkv_kernel,
        grid=(1,),
        in_specs=[spec, _const_spec(wts['mem_norm'].shape), _const_spec(wts['w_mk'].shape),
                  _const_spec(wts['w_mv'].shape)],
        out_specs=[spec, spec, spec, spec],
        out_shape=[full(F32), full(F32), full(BF16), full(BF16)],
        compiler_params=pltpu.CompilerParams(dimension_semantics=("arbitrary",),
                                             vmem_limit_bytes=VMEM_LIMIT_BYTES),
        name="memkv",
    )(mem, wts['mem_norm'], wts['w_mk'], wts['w_mv'])


def _prep_weights(p):
    w = {}
    rowv = lambda a: a.reshape(1, -1).astype(F32)
    for name in ('ffn1_norm', 'mix_norm', 'q_norm', 'kv_norm', 'xattn_norm', 'ffn2_norm', 'final_norm',
                 'mem_norm', 'lru_out_norm', 'conv_b', 'lru_lambda'):
        w[name] = rowv(p[name])
    for name in ('ffn1_w1', 'ffn1_w3', 'ffn1_w2', 'ffn2_w1', 'ffn2_w3', 'ffn2_w2', 'w_mq', 'w_mk', 'w_mv',
                 'w_mo'):
        w[name] = p[name].astype(BF16)

    half = QK_ROPE // 2
    w_in = p['w_in']
    kpe_cols = w_in[:, ZC_KPE:ZC_KPE + QK_ROPE]
    pad_l = jnp.zeros((D_MODEL, QK_NOPE), F32)
    pad_r = jnp.zeros((D_MODEL, HEAD_PAD - QK_NOPE - QK_ROPE), F32)
    kpe_blk = jnp.concatenate([pad_l, kpe_cols, pad_r], axis=1)
    kpe_rot = jnp.concatenate([pad_l, kpe_cols[:, half:], kpe_cols[:, :half], pad_r], axis=1)
    lru_cols = w_in[:, Q_LORA + KV_LORA + QK_ROPE:]
    w['w_in_ext'] = jnp.concatenate([w_in[:, :Q_LORA + KV_LORA], kpe_blk, kpe_rot, lru_cols], axis=1).astype(BF16)

    wq = p['w_uq'].reshape(Q_LORA, MLA_HEADS, QK_NOPE + QK_ROPE)
    zq = lambda n: jnp.zeros((Q_LORA, MLA_HEADS, n), F32)
    q_main = jnp.concatenate([wq, zq(HEAD_PAD - QK_NOPE - QK_ROPE)], axis=2)
    q_rot = jnp.concatenate([zq(QK_NOPE), wq[:, :, QK_NOPE + half:], wq[:, :, QK_NOPE:QK_NOPE + half],
                             zq(HEAD_PAD - QK_NOPE - QK_ROPE)], axis=2)
    w['w_uq_ext'] = jnp.concatenate([q_main.reshape(Q_LORA, MLA_PAD), q_rot.reshape(Q_LORA, MLA_PAD)],
                                    axis=1).astype(BF16)

    wkv = p['w_ukv'].reshape(KV_LORA, MLA_HEADS, QK_NOPE + V_HEAD)
    zk = jnp.zeros((KV_LORA, MLA_HEADS, HEAD_PAD - QK_NOPE), F32)
    zv = jnp.zeros((KV_LORA, MLA_HEADS, HEAD_PAD - V_HEAD), F32)
    k_part = jnp.concatenate([wkv[:, :, :QK_NOPE], zk], axis=2).reshape(KV_LORA, MLA_PAD)
    v_part = jnp.concatenate([wkv[:, :, QK_NOPE:], zv], axis=2).reshape(KV_LORA, MLA_PAD)
    w['w_ukv_ext'] = jnp.concatenate([k_part, v_part], axis=1).astype(BF16)
    vt = jnp.concatenate([wkv[:, :, QK_NOPE:], jnp.zeros((KV_LORA, MLA_HEADS, VT_HEAD - V_HEAD), F32)], axis=2)
    w['w_uvt_ext'] = vt.reshape(KV_LORA, VT_ROWS).T.astype(BF16)

    inv_freq = ROPE_THETA ** (-jnp.arange(0, QK_ROPE, 2, dtype=F32) / QK_ROPE)
    zl = jnp.zeros((QK_NOPE,), F32)
    zr = jnp.zeros((HEAD_PAD - QK_NOPE - QK_ROPE,), F32)
    w['rope_freq'] = jnp.concatenate([zl, inv_freq, inv_freq, zr]).reshape(1, HEAD_PAD)
    w['rope_sign'] = jnp.concatenate([zl, -jnp.ones((half,), F32), jnp.ones((half,), F32), zr]).reshape(1, HEAD_PAD)

    g_attn = jnp.concatenate([p['attn_out_norm'].reshape(MLA_HEADS, V_HEAD),
                              jnp.zeros((MLA_HEADS, HEAD_PAD - V_HEAD), F32)], axis=1)
    w['attn_out_norm_pad'] = g_attn.reshape(1, MLA_PAD)
    w['attn_out_norm'] = rowv(p['attn_out_norm'])

    w_out = p['w_out']
    wo_attn = jnp.concatenate([w_out[:MLA_WIDTH].reshape(MLA_HEADS, V_HEAD, D_MODEL),
                               jnp.zeros((MLA_HEADS, HEAD_PAD - V_HEAD, D_MODEL), F32)], axis=1)
    w['w_out_attn_pad'] = wo_attn.reshape(MLA_PAD, D_MODEL).astype(BF16)
    w['w_out_attn'] = w_out[:MLA_WIDTH].astype(BF16)
    w['w_out_lru'] = w_out[MLA_WIDTH:].astype(BF16)

    def block_diag(wb):
        eye = jnp.eye(LRU_BLOCKS, dtype=F32)
        return (eye[:, None, :, None] * wb[:, :, None, :]).reshape(LRU_WIDTH, LRU_WIDTH)

    w['lru_wg'] = jnp.concatenate([block_diag(p['lru_wa']), block_diag(p['lru_wx'])], axis=1).astype(BF16)
    w['lru_bg'] = jnp.concatenate([p['lru_ba'].reshape(1, -1), p['lru_bx'].reshape(1, -1)], axis=1).astype(F32)
    w['conv_w'] = p['conv_w'].astype(F32)
    return w


def _pick_tile(n, cap):
    t = min(n, cap)
    while n % t:
        t //= 2
    return t


def _group(x, mk, mv, wts, *, past, conv_init, h_init):
    nbatch, seq, _ = x.shape
    n = nbatch * seq
    xf = x.reshape(n, D_MODEL)
    tm = _pick_tile(n, 512)
    ff_chunk = D_FF // 2
    pos_base = 0 if past is None else past[0].shape[1]
    x1, q, k, v, ckv, kpe, lru, cstate, hstate = _head_call(
        xf, conv_init, h_init, wts, seq_len=seq, pos_base=pos_base, tm=tm, ff_chunk=ff_chunk,
        v_transposed=past is None)
    if past is None:
        assert nbatch == 1
        tq = _pick_tile(n, 1024)
        attn = _mla_prompt_call(q, k, v, wts['attn_out_norm'], tq=tq, tk=_pick_tile(tq, 512))
    else:
        ckv_past, kpe_past = past
        kpe_pad = jnp.pad(kpe_past, ((0, 0), (0, 0), (QK_NOPE, HEAD_PAD - QK_NOPE - QK_ROPE)))
        attn = _mla_sample_call(q, k, v, ckv_past, kpe_pad, wts['w_ukv_ext'], wts['attn_out_norm_pad'],
                                t_new=seq)
    nb = tm // seq if seq < tm else 1
    y = _tail_call(x1, attn, lru, mk, mv, wts, seq_len=seq, tm=tm, nb=nb, ff_chunk=ff_chunk)
    return (y.reshape(nbatch, seq, D_MODEL), ckv.reshape(nbatch, seq, KV_LORA),
            kpe.reshape(nbatch, seq, QK_ROPE), cstate[:, SUBLANES - (CONV_W - 1):, :], hstate[:, 0, :])


def kernel(x_prompt, x_sample, mem_prompt, cache_mla_ckv, cache_mla_kpe, state_conv, state_lru, cache_mem_k, cache_mem_v, ffn1_norm, ffn1_w1, ffn1_w3, ffn1_w2, mix_norm, w_in, q_norm, w_uq, kv_norm, w_ukv, conv_w, conv_b, lru_wa, lru_ba, lru_wx, lru_bx, lru_lambda, attn_out_norm, lru_out_norm, w_out, mem_norm, xattn_norm, w_mq, w_mk, w_mv, w_mo, ffn2_norm, ffn2_w1, ffn2_w3, ffn2_w2, final_norm):
    depth = ffn1_norm.shape[0]
    assert depth == 1
    params = dict(ffn1_norm=ffn1_norm[0], ffn1_w1=ffn1_w1[0], ffn1_w3=ffn1_w3[0], ffn1_w2=ffn1_w2[0],
                  mix_norm=mix_norm[0], w_in=w_in[0], q_norm=q_norm[0], w_uq=w_uq[0], kv_norm=kv_norm[0],
                  w_ukv=w_ukv[0], conv_w=conv_w[0], conv_b=conv_b[0], lru_wa=lru_wa[0], lru_ba=lru_ba[0],
                  lru_wx=lru_wx[0], lru_bx=lru_bx[0], lru_lambda=lru_lambda[0],
                  attn_out_norm=attn_out_norm[0], lru_out_norm=lru_out_norm[0], w_out=w_out[0],
                  mem_norm=mem_norm[0], xattn_norm=xattn_norm[0], w_mq=w_mq[0], w_mk=w_mk[0], w_mv=w_mv[0],
                  w_mo=w_mo[0], ffn2_norm=ffn2_norm[0], ffn2_w1=ffn2_w1[0], ffn2_w3=ffn2_w3[0],
                  ffn2_w2=ffn2_w2[0], final_norm=final_norm)
    wts = _prep_weights(params)

    bp = x_prompt.shape[0]
    bs = x_sample.shape[0]
    n_mem = mem_prompt.shape[1]

    assert bp == 1
    mk_f, mv_f, mk_b, mv_b = _memkv_call(mem_prompt.reshape(bp * n_mem, D_MODEL), wts)
    zero_state = jnp.zeros((bp, SUBLANES, LRU_WIDTH), F32)
    y_p, ckv_p, kpe_p, conv_p, lru_p = _group(
        x_prompt, mk_b.reshape(bp, n_mem, D_MODEL), mv_b.reshape(bp, n_mem, D_MODEL), wts,
        past=None, conv_init=zero_state, h_init=zero_state)

    conv_init = jnp.pad(state_conv[0], ((0, 0), (SUBLANES - (CONV_W - 1), 0), (0, 0)))
    h_init = jnp.broadcast_to(state_lru[0][:, None, :], (bs, SUBLANES, LRU_WIDTH))
    y_s, ckv_s, kpe_s, conv_s, lru_s = _group(
        x_sample, cache_mem_k[0].reshape(bs, n_mem, D_MODEL).astype(BF16),
        cache_mem_v[0].reshape(bs, n_mem, D_MODEL).astype(BF16), wts,
        past=(cache_mla_ckv[0], cache_mla_kpe[0]), conv_init=conv_init, h_init=h_init)

    mem_shape = (1, bp, n_mem, MEM_HEADS, MEM_HEAD_DIM)
    return (y_p, y_s, ckv_p[None], kpe_p[None], conv_p[None], lru_p[None],
            mk_f.reshape(mem_shape), mv_f.reshape(mem_shape),
            ckv_s[None], kpe_s[None], conv_s[None], lru_s[None])
```

```python
import functools

import jax
import jax.numpy as jnp
from jax import lax
from jax.experimental import pallas as pl
from jax.experimental.pallas import tpu as pltpu

F32 = jnp.float32
BF16 = jnp.bfloat16

D_MODEL = 1024
CHUNK = 64
N_MEM = 256
MLA_HEADS = 8
Q_LORA = 384
KV_LORA = 256
QK_NOPE = 64
QK_ROPE = 32
V_HEAD = 64
MLA_WIDTH = MLA_HEADS * V_HEAD
MLA_SCALE = (QK_NOPE + QK_ROPE) ** -0.5
ROPE_THETA = 10000.0
LRU_WIDTH = 512
LRU_BLOCKS = 8
LRU_BLOCK = LRU_WIDTH // LRU_BLOCKS
CONV_W = 4
LRU_C = 8.0
MEM_HEADS = 4
MEM_HEAD_DIM = D_MODEL // MEM_HEADS
D_FF = 2816
EPS = 1e-6

LANES = 128
SUBLANES = 8
HEAD_PAD = LANES
MLA_PAD = MLA_HEADS * HEAD_PAD
VT_HEAD = V_HEAD + 16
VT_ROWS = MLA_HEADS * VT_HEAD
Q_SCALE = MLA_SCALE * 1.4426950408889634
QK_LOOKAHEAD = 2
ROW_STREAMS = 2
RESCALE_THRESHOLD = 64.0
FIRST_BLOCK_FLOOR = 64.0
ZC_Q = 0
ZC_KV = ZC_Q + Q_LORA
ZC_KPE = ZC_KV + KV_LORA
ZC_KPE_ROT = ZC_KPE + HEAD_PAD
ZC_XBR = ZC_KPE_ROT + HEAD_PAD
ZC_GBR = ZC_XBR + LRU_WIDTH
Z_COLS = ZC_GBR + LRU_WIDTH
NEG_BIG = -1e30
VMEM_LIMIT_BYTES = 56 * 1024 * 1024


def _const_spec(shape):
    nd = len(shape)
    return pl.BlockSpec(shape, lambda *_: (0,) * nd, pipeline_mode=pl.Buffered(1))


def _rms(x, g):
    ms = jnp.mean(x * x, axis=-1, keepdims=True)
    return x * lax.rsqrt(ms + EPS) * g


def _dot(a, b):
    return jnp.dot(a, b, preferred_element_type=F32)


def _dot_nt(a, b):
    return lax.dot_general(a, b, (((1,), (1,)), ((), ())), preferred_element_type=F32)


def _swiglu_half(xs, g_ref, w1_ref, w3_ref, w2_ref, ff_chunk):
    hs = [_rms(x, g_ref[...]).astype(BF16) for x in xs]
    accs = [jnp.zeros(x.shape, F32) for x in xs]
    for c in range(D_FF // ff_chunk):
        lo = c * ff_chunk
        ups = [(_dot(h, w1_ref[:, lo:lo + ff_chunk]), _dot(h, w3_ref[:, lo:lo + ff_chunk])) for h in hs]
        gs = [(a * jax.nn.sigmoid(a) * b).astype(BF16) for a, b in ups]
        accs = [acc + _dot(g, w2_ref[lo:lo + ff_chunk, :]) for acc, g in zip(accs, gs)]
    return [x + 0.5 * acc for x, acc in zip(xs, accs)]


def _row_streams(tm, n_streams):
    size = tm // n_streams
    return [(s * size, size) for s in range(n_streams)]


def _head_kernel(x_ref, cinit_ref, hinit_ref, g1_ref, w1_ref, w3_ref, w2_ref, gmix_ref, win_ref, gq_ref,
                 wuq_ref, gkv_ref, wukv_ref, wuvt_ref, freq_ref, sign_ref,
                 convw_ref, convb_ref, wg_ref, bg_ref, lam_ref, gl_ref,
                 x1_ref, q_ref, k_ref, v_ref, ckv_ref, kpe_ref, lru_ref, cstate_ref, hstate_ref,
                 xe_scr, ccar_scr, hcar_scr, zbuf_scr,
                 *, tm, seq_len, pos_base, ff_chunk, v_transposed, n_tiles, pipelined, n_streams):
    step = pl.program_id(0)
    lru_consts = (convw_ref, convb_ref, wg_ref, bg_ref, lam_ref, gl_ref)
    bcast = lambda hrow: jnp.broadcast_to(hrow, (SUBLANES, LRU_WIDTH))
    if pipelined:
        @pl.when(step == 0)
        def _():
            zbuf_scr[...] = jnp.zeros(zbuf_scr.shape, F32)
            ccar_scr[...] = jnp.zeros(ccar_scr.shape, F32)
            hcar_scr[...] = jnp.zeros(hcar_scr.shape, F32)

        new_seq = lax.rem(step - 1, seq_len // tm) == 0
        conv_prev = jnp.where(new_seq, cinit_ref[0], ccar_scr[...])
        h_prev = jnp.where(new_seq, hinit_ref[0, 0:1, :], hcar_scr[0:1, :])
        out, tail, h_last = _lru_block(zbuf_scr[:, :LRU_WIDTH], zbuf_scr[:, LRU_WIDTH:], conv_prev, h_prev,
                                       xe_scr, *lru_consts)
        lru_ref[...] = out
        ccar_scr[...] = tail
        hcar_scr[...] = bcast(h_last)
        cstate_ref[0] = tail
        hstate_ref[0] = bcast(h_last)
        tile = jnp.minimum(step, n_tiles - 1)
    else:
        tile = step

    streams = _row_streams(tm, n_streams)
    x1 = _swiglu_half([x_ref[lo:lo + sz, :] for lo, sz in streams], g1_ref, w1_ref, w3_ref, w2_ref, ff_chunk)
    x1 = x1[0] if n_streams == 1 else jnp.concatenate(x1, axis=0)
    x1_ref[...] = x1

    h = _rms(x1, gmix_ref[...]).astype(BF16)
    z = _dot(h, win_ref[...])

    row = lax.broadcasted_iota(jnp.int32, (tm, HEAD_PAD), 0) + tile * tm
    pos = (row & (seq_len - 1)) + pos_base
    ang = pos.astype(F32) * freq_ref[...]
    cos = jnp.cos(ang)
    sin = jnp.sin(ang) * sign_ref[...]

    cqn = _rms(z[:, ZC_Q:ZC_Q + Q_LORA], gq_ref[...]).astype(BF16)
    qz = _dot(cqn, wuq_ref[...])
    cos_q = cos * Q_SCALE
    sin_q = sin * Q_SCALE
    for hd in range(MLA_HEADS):
        lo = hd * HEAD_PAD
        qh = qz[:, lo:lo + HEAD_PAD] * cos_q + qz[:, MLA_PAD + lo:MLA_PAD + lo + HEAD_PAD] * sin_q
        q_ref[:, lo:lo + HEAD_PAD] = qh.astype(BF16)

    ckv = _rms(z[:, ZC_KV:ZC_KV + KV_LORA], gkv_ref[...])
    ckv_ref[...] = ckv
    kpe_blk = z[:, ZC_KPE:ZC_KPE + HEAD_PAD] * cos + z[:, ZC_KPE_ROT:ZC_KPE_ROT + HEAD_PAD] * sin
    kpe_ref[...] = kpe_blk[:, QK_NOPE:QK_NOPE + QK_ROPE]

    ckv_b = ckv.astype(BF16)
    kz = _dot(ckv_b, wukv_ref[:, :MLA_PAD])
    for hd in range(MLA_HEADS):
        lo = hd * HEAD_PAD
        k_ref[:, lo:lo + HEAD_PAD] = (kz[:, lo:lo + HEAD_PAD] + kpe_blk).astype(BF16)
    if v_transposed:
        vt = _dot_nt(wuvt_ref[...], ckv_b)
        ridx = lax.broadcasted_iota(jnp.int32, vt.shape, 0)
        is_one = ridx == V_HEAD
        for hd in range(1, MLA_HEADS):
            is_one = is_one | (ridx == hd * VT_HEAD + V_HEAD)
        v_ref[...] = jnp.where(is_one, 1.0, vt).astype(BF16)
    else:
        v_ref[...] = _dot(ckv_b, wukv_ref[:, MLA_PAD:]).astype(BF16)

    xbr = z[:, ZC_XBR:ZC_XBR + LRU_WIDTH]
    gbr = z[:, ZC_GBR:ZC_GBR + LRU_WIDTH]
    if pipelined:
        zbuf_scr[:, :LRU_WIDTH] = xbr
        zbuf_scr[:, LRU_WIDTH:] = gbr
    elif seq_len >= tm:
        @pl.when(lax.rem(step, seq_len // tm) == 0)
        def _():
            ccar_scr[...] = cinit_ref[0]
            hcar_scr[...] = hinit_ref[0]

        out, tail, h_last = _lru_block(xbr, gbr, ccar_scr[...], hcar_scr[0:1, :], xe_scr, *lru_consts)
        ccar_scr[...] = tail
        hcar_scr[...] = bcast(h_last)
        lru_ref[...] = out
        cstate_ref[0] = tail
        hstate_ref[0] = bcast(h_last)
    else:
        for b in range(tm // seq_len):
            lo = b * seq_len
            out, tail, h_last = _lru_block(xbr[lo:lo + seq_len, :], gbr[lo:lo + seq_len, :], cinit_ref[b],
                                           hinit_ref[b, 0:1, :], xe_scr, *lru_consts)
            lru_ref[lo:lo + seq_len, :] = out
            cstate_ref[b] = tail
            hstate_ref[b] = bcast(h_last)


def _head_call(x, conv_init, h_init, wts, *, seq_len, pos_base, tm, ff_chunk, v_transposed):
    n = x.shape[0]
    nseq = n // seq_len
    assert n % tm == 0 and seq_len & (seq_len - 1) == 0 and seq_len >= SUBLANES
    assert seq_len % tm == 0 or tm % seq_len == 0
    nsub = max(1, tm // seq_len)
    n_tiles = n // tm
    pipelined = seq_len > tm
    if pipelined:
        row = lambda w: pl.BlockSpec((tm, w), lambda i: (jnp.minimum(i, n_tiles - 1), 0))
        lru_tile = lambda i: jnp.maximum(i - 1, 0)
        state = pl.BlockSpec((1, SUBLANES, LRU_WIDTH), lambda i: ((lru_tile(i) * tm) // seq_len, 0, 0))
    else:
        row = lambda w: pl.BlockSpec((tm, w), lambda i: (i, 0))
        state = pl.BlockSpec((nsub, SUBLANES, LRU_WIDTH), lambda i: ((i * tm) // (seq_len * nsub), 0, 0))
    consts = [wts['ffn1_norm'], wts['ffn1_w1'], wts['ffn1_w3'], wts['ffn1_w2'], wts['mix_norm'],
              wts['w_in_ext'], wts['q_norm'], wts['w_uq_ext'], wts['kv_norm'], wts['w_ukv_ext'],
              wts['w_uvt_ext'], wts['rope_freq'], wts['rope_sign'],
              wts['conv_w'], wts['conv_b'], wts['lru_wg'], wts['lru_bg'], wts['lru_lambda'],
              wts['lru_out_norm']]
    v_shape = (VT_ROWS, n) if v_transposed else (n, MLA_PAD)
    if v_transposed:
        v_spec = pl.BlockSpec((VT_ROWS, tm), lambda i: (0, jnp.minimum(i, n_tiles - 1)))
    else:
        v_spec = row(MLA_PAD)
    state_shape = jax.ShapeDtypeStruct((nseq, SUBLANES, LRU_WIDTH), F32)
    out_shapes = [
        jax.ShapeDtypeStruct((n, D_MODEL), F32),
        jax.ShapeDtypeStruct((n, MLA_PAD), BF16),
        jax.ShapeDtypeStruct((n, MLA_PAD), BF16),
        jax.ShapeDtypeStruct(v_shape, BF16),
        jax.ShapeDtypeStruct((n, KV_LORA), F32),
        jax.ShapeDtypeStruct((n, QK_ROPE), F32),
        jax.ShapeDtypeStruct((n, LRU_WIDTH), BF16),
        state_shape,
        state_shape,
    ]
    out_specs = [row(s.shape[1]) for s in out_shapes[:7]] + [state, state]
    out_specs[3] = v_spec
    if pipelined:
        out_specs[6] = pl.BlockSpec((tm, LRU_WIDTH), lambda i: (lru_tile(i), 0))
    lru_rows = min(tm, seq_len)
    return pl.pallas_call(
        functools.partial(_head_kernel, tm=tm, seq_len=seq_len, pos_base=pos_base, ff_chunk=ff_chunk,
                          v_transposed=v_transposed, n_tiles=n_tiles, pipelined=pipelined,
                          n_streams=ROW_STREAMS if tm % (ROW_STREAMS * 16) == 0 and tm >= 512 else 1),
        grid=(n_tiles + 1 if pipelined else n_tiles,),
        in_specs=[row(D_MODEL), state, state] + [_const_spec(c.shape) for c in consts],
        out_specs=out_specs,
        out_shape=out_shapes,
        scratch_shapes=[pltpu.VMEM((lru_rows + SUBLANES, LRU_WIDTH), F32),
                        pltpu.VMEM((SUBLANES, LRU_WIDTH), F32),
                        pltpu.VMEM((SUBLANES, LRU_WIDTH), F32),
                        pltpu.VMEM((tm if pipelined else SUBLANES, 2 * LRU_WIDTH), F32)],
        compiler_params=pltpu.CompilerParams(dimension_semantics=("arbitrary",),
                                             vmem_limit_bytes=VMEM_LIMIT_BYTES),
        name="head",
    )(x, conv_init, h_init, *consts)


def _attn_finalize(o_heads, gattn_ref, out_ref):
    ss = o_heads[0] * o_heads[0]
    for o in o_heads[1:]:
        ss = ss + o * o
    ms = jnp.sum(ss, axis=-1, keepdims=True) * (1.0 / MLA_WIDTH)
    inv = lax.rsqrt(ms + EPS)
    for hd, o in enumerate(o_heads):
        lo = hd * HEAD_PAD
        out_ref[:, lo:lo + HEAD_PAD] = (o * inv * gattn_ref[:, lo:lo + HEAD_PAD]).astype(BF16)


def _mla_prompt_step(q_ref, k_ref, vt_ref, m_scr, mx_scr, acc_scr, cur, first, visible, *, rescale):
    def scores(hd):
        lo = hd * HEAD_PAD
        return _dot_nt(k_ref[:, lo:lo + HEAD_PAD], q_ref[:, lo:lo + HEAD_PAD])

    pending = [scores(hd) for hd in range(QK_LOOKAHEAD)]
    for hd in range(MLA_HEADS):
        st = pending[hd]
        if hd + QK_LOOKAHEAD < MLA_HEADS:
            pending.append(scores(hd + QK_LOOKAHEAD))
        if visible is not None:
            st = jnp.where(visible, st, NEG_BIG)
        vo = hd * VT_HEAD
        blk_max = jnp.max(st, axis=0, keepdims=True)
        if rescale:
            m_prev = jnp.where(first, NEG_BIG, m_scr[hd:hd + 1, :])
            m_new = jnp.maximum(m_prev, blk_max)
            alpha = jnp.exp2(m_prev - m_new)
            pt = jnp.exp2(st - m_new).astype(BF16)
            acc_scr[cur, vo:vo + VT_HEAD, :] = (alpha * acc_scr[cur, vo:vo + VT_HEAD, :]
                                                + _dot(vt_ref[vo:vo + VT_HEAD, :], pt))
            m_scr[hd:hd + 1, :] = m_new
        else:
            mx_scr[hd:hd + 1, :] = blk_max
            pt = jnp.exp2(st - m_scr[hd:hd + 1, :]).astype(BF16)
            acc_scr[1 - cur, vo:vo + VT_HEAD, :] = (acc_scr[cur, vo:vo + VT_HEAD, :]
                                                    + _dot(vt_ref[vo:vo + VT_HEAD, :], pt))


def _mla_prompt_kernel(qi_ref, ki_ref, q_ref, k_ref, vt_ref, gattn_ref, out_ref,
                       m_scr, mx_scr, acc_scr, cur_scr, *, tq, tk):
    step = pl.program_id(0)
    q_lo = qi_ref[step] * tq
    k_lo = ki_ref[step] * tk
    first = k_lo == 0

    @pl.when(first)
    def _():
        m_scr[...] = jnp.zeros(m_scr.shape, F32)
        acc_scr[0] = jnp.zeros(acc_scr.shape[1:], F32)
        cur_scr[0] = 0

    cur = cur_scr[0]
    fully_visible = k_lo + tk <= q_lo + CHUNK

    def chunk_mask():
        kc = (lax.broadcasted_iota(jnp.int32, (tk, tq), 0) + k_lo) // CHUNK
        qc = (lax.broadcasted_iota(jnp.int32, (tk, tq), 1) + q_lo) // CHUNK
        return kc <= qc

    step_fn = functools.partial(_mla_prompt_step, q_ref, k_ref, vt_ref, m_scr, mx_scr, acc_scr, cur, first)

    @pl.when(fully_visible)
    def _():
        step_fn(None, rescale=False)

    @pl.when(jnp.logical_not(fully_visible))
    def _():
        step_fn(chunk_mask(), rescale=False)

    excess = mx_scr[...] - m_scr[...]
    redo = (jnp.max(excess) > RESCALE_THRESHOLD) | (first & (jnp.min(excess) < -FIRST_BLOCK_FLOOR))

    @pl.when(redo)
    def _():
        step_fn(chunk_mask(), rescale=True)

    @pl.when(jnp.logical_not(redo))
    def _():
        cur_scr[0] = 1 - cur

    @pl.when(k_lo + tk >= q_lo + tq)
    def _():
        fin = cur_scr[0]
        parts = []
        for hd in range(MLA_HEADS):
            vo = hd * VT_HEAD
            parts.append(acc_scr[fin, vo:vo + V_HEAD, :] / acc_scr[fin, vo + V_HEAD:vo + V_HEAD + 1, :])
        o = jnp.concatenate(parts, axis=0).T
        out_ref[...] = _rms(o, gattn_ref[...]).astype(BF16)


def _mla_prompt_call(q, k, vt, gattn, *, tq, tk):
    n = q.shape[0]
    assert n % tq == 0 and tq % tk == 0 and tk % CHUNK == 0
    pairs = [(qi, ki) for qi in range(n // tq) for ki in range((qi + 1) * tq // tk)]
    qi_tab = jnp.asarray([p[0] for p in pairs], jnp.int32)
    ki_tab = jnp.asarray([p[1] for p in pairs], jnp.int32)
    grid_spec = pltpu.PrefetchScalarGridSpec(
        num_scalar_prefetch=2,
        grid=(len(pairs),),
        in_specs=[pl.BlockSpec((tq, MLA_PAD), lambda s, qi, ki: (qi[s], 0)),
                  pl.BlockSpec((tk, MLA_PAD), lambda s, qi, ki: (ki[s], 0)),
                  pl.BlockSpec((VT_ROWS, tk), lambda s, qi, ki: (0, ki[s])),
                  pl.BlockSpec(gattn.shape, lambda s, qi, ki: (0, 0))],
        out_specs=pl.BlockSpec((tq, MLA_WIDTH), lambda s, qi, ki: (qi[s], 0)),
        scratch_shapes=[pltpu.VMEM((MLA_HEADS, tq), F32),
                        pltpu.VMEM((MLA_HEADS, tq), F32),
                        pltpu.VMEM((2, VT_ROWS, tq), F32),
                        pltpu.SMEM((1,), jnp.int32)],
    )
    return pl.pallas_call(
        functools.partial(_mla_prompt_kernel, tq=tq, tk=tk),
        grid_spec=grid_spec,
        out_shape=jax.ShapeDtypeStruct((n, MLA_WIDTH), BF16),
        compiler_params=pltpu.CompilerParams(dimension_semantics=("arbitrary",),
                                             vmem_limit_bytes=VMEM_LIMIT_BYTES),
        name="mla_prompt",
    )(qi_tab, ki_tab, q, k, vt, gattn)


def _mla_sample_kernel(q_ref, kn_ref, vn_ref, ckv_ref, kpe_ref, wukv_ref, gattn_ref, out_ref):
    cp = ckv_ref[0].astype(BF16)
    kpe = kpe_ref[0]
    o_heads = []
    for hd in range(MLA_HEADS):
        lo = hd * HEAD_PAD
        qh = q_ref[:, lo:lo + HEAD_PAD]
        kp = (_dot(cp, wukv_ref[:, lo:lo + HEAD_PAD]) + kpe).astype(BF16)
        vp = _dot(cp, wukv_ref[:, MLA_PAD + lo:MLA_PAD + lo + HEAD_PAD]).astype(BF16)
        s1 = _dot_nt(qh, kp)
        s2 = _dot_nt(qh, kn_ref[:, lo:lo + HEAD_PAD])
        m = jnp.maximum(jnp.max(s1, axis=-1, keepdims=True), jnp.max(s2, axis=-1, keepdims=True))
        p1 = jnp.exp2(s1 - m)
        p2 = jnp.exp2(s2 - m)
        l = jnp.sum(p1, axis=-1, keepdims=True) + jnp.sum(p2, axis=-1, keepdims=True)
        o = _dot(p1.astype(BF16), vp) + _dot(p2.astype(BF16), vn_ref[:, lo:lo + HEAD_PAD])
        o_heads.append(o / l)
    _attn_finalize(o_heads, gattn_ref, out_ref)


def _mla_sample_call(q, k, v, ckv_past, kpe_past_pad, w_ukv_ext, gattn, *, t_new):
    n = q.shape[0]
    nbatch, past, _ = ckv_past.shape
    assert n == nbatch * t_new
    rows = pl.BlockSpec((t_new, MLA_PAD), lambda b: (b, 0))
    return pl.pallas_call(
        _mla_sample_kernel,
        grid=(nbatch,),
        in_specs=[rows, rows, rows,
                  pl.BlockSpec((1, past, KV_LORA), lambda b: (b, 0, 0)),
                  pl.BlockSpec((1, past, HEAD_PAD), lambda b: (b, 0, 0)),
                  _const_spec(w_ukv_ext.shape), _const_spec(gattn.shape)],
        out_specs=rows,
        out_shape=jax.ShapeDtypeStruct((n, MLA_PAD), BF16),
        compiler_params=pltpu.CompilerParams(dimension_semantics=("arbitrary",),
                                             vmem_limit_bytes=VMEM_LIMIT_BYTES),
        name="mla_sample",
    )(q, k, v, ckv_past, kpe_past_pad, w_ukv_ext, gattn)


def _lru_block(x, gate, conv_prev, h_prev, xe_scr, convw_ref, convb_ref, wg_ref, bg_ref, lam_ref, gl_ref):
    tm = x.shape[0]
    xe_scr[0:SUBLANES, :] = conv_prev
    xe_scr[SUBLANES:SUBLANES + tm, :] = x
    xc = convb_ref[...] + xe_scr[pl.ds(SUBLANES - 3, tm), :] * convw_ref[0:1, :]
    xc = xc + xe_scr[pl.ds(SUBLANES - 2, tm), :] * convw_ref[1:2, :]
    xc = xc + xe_scr[pl.ds(SUBLANES - 1, tm), :] * convw_ref[2:3, :]
    xc = xc + x * convw_ref[3:4, :]
    tail = x[tm - SUBLANES:, :]

    y = _dot(xc.astype(BF16), wg_ref[...]) + bg_ref[...]
    r = jax.nn.sigmoid(y[:, :LRU_WIDTH])
    ig = jax.nn.sigmoid(y[:, LRU_WIDTH:])
    nl = -lam_ref[...]
    softplus = jnp.maximum(nl, 0.0) + jnp.log1p(jnp.exp(-jnp.abs(nl)))
    log_a = -LRU_C * r * softplus
    a = jnp.exp(log_a)
    th = jnp.tanh(log_a)
    one_minus_a2 = -2.0 * th / (1.0 - th)
    b = jnp.sqrt(one_minus_a2) * ig * xc

    sub = lax.broadcasted_iota(jnp.int32, (tm, LRU_WIDTH), 0) & (SUBLANES - 1)
    s = 1
    while s < SUBLANES:
        valid = sub >= s
        a_sh = pltpu.roll(a, s, axis=0)
        b_sh = pltpu.roll(b, s, axis=0)
        b = jnp.where(valid, a * b_sh + b, b)
        a = jnp.where(valid, a * a_sh, a)
        s *= 2
    h_groups = []
    for g in range(tm // SUBLANES):
        lo = g * SUBLANES
        hg = b[lo:lo + SUBLANES, :] + a[lo:lo + SUBLANES, :] * h_prev
        h_groups.append(hg)
        h_prev = hg[SUBLANES - 1:SUBLANES, :]
    h = jnp.concatenate(h_groups, axis=0)
    out = _rms(jax.nn.gelu(gate) * h, gl_ref[...]).astype(BF16)
    return out, tail, h_prev


def _tail_kernel(x1_ref, attn_ref, lru_ref, woa_ref, wol_ref, gx_ref, wmq_ref, mk_ref, mv_ref, wmo_ref,
                 g2_ref, w1_ref, w3_ref, w2_ref, gf_ref, y_ref, *, tm, nb, ff_chunk, n_streams):
    streams = _row_streams(tm, n_streams)
    x2s = [x1_ref[lo:lo + sz, :] + _dot(attn_ref[lo:lo + sz, :], woa_ref[...])
           + _dot(lru_ref[lo:lo + sz, :], wol_ref[...]) for lo, sz in streams]

    hqs = [_rms(x2, gx_ref[...]).astype(BF16) for x2 in x2s]
    qs = [(_dot(hq, wmq_ref[...]) * (MEM_HEAD_DIM ** -0.5)).astype(BF16) for hq in hqs]

    def cross_attend(q, mem_batch):
        o_heads = []
        for hd in range(MEM_HEADS):
            lo = hd * MEM_HEAD_DIM
            s = _dot_nt(q[:, lo:lo + MEM_HEAD_DIM], mk_ref[mem_batch, :, lo:lo + MEM_HEAD_DIM])
            p = jnp.exp(s - jnp.max(s, axis=-1, keepdims=True))
            l = jnp.sum(p, axis=-1, keepdims=True)
            o_heads.append(_dot(p.astype(BF16), mv_ref[mem_batch, :, lo:lo + MEM_HEAD_DIM]) / l)
        return jnp.concatenate(o_heads, axis=1)

    if nb == 1:
        os_ = [cross_attend(q, 0) for q in qs]
    else:
        rows = tm // nb
        os_ = [jnp.concatenate([cross_attend(qs[0][b * rows:(b + 1) * rows, :], b) for b in range(nb)], axis=0)]
    x3s = [x2 + _dot(o.astype(BF16), wmo_ref[...]) for x2, o in zip(x2s, os_)]

    x4s = _swiglu_half(x3s, g2_ref, w1_ref, w3_ref, w2_ref, ff_chunk)
    for (lo, sz), x4 in zip(streams, x4s):
        y_ref[lo:lo + sz, :] = _rms(x4, gf_ref[...])


def _tail_call(x1, attn, lru, mk, mv, wts, *, seq_len, tm, nb, ff_chunk):
    n = x1.shape[0]
    assert n % tm == 0 and tm % nb == 0 and mk.shape[0] * seq_len == n
    assert (nb == 1 and seq_len % tm == 0) or nb * seq_len == tm
    row = lambda w: pl.BlockSpec((tm, w), lambda i: (i, 0))
    mem = pl.BlockSpec((nb, N_MEM, D_MODEL), lambda i: ((i * tm) // (seq_len * nb), 0, 0))
    c = lambda name: _const_spec(wts[name].shape)
    w_out_attn = wts['w_out_attn'] if attn.shape[1] == MLA_WIDTH else wts['w_out_attn_pad']
    return pl.pallas_call(
        functools.partial(_tail_kernel, tm=tm, nb=nb, ff_chunk=ff_chunk,
                          n_streams=ROW_STREAMS if nb == 1 and tm % (ROW_STREAMS * 16) == 0 else 1),
        grid=(n // tm,),
        in_specs=[row(D_MODEL), row(attn.shape[1]), row(LRU_WIDTH), _const_spec(w_out_attn.shape),
                  c('w_out_lru'), c('xattn_norm'), c('w_mq'),
                  mem, mem, c('w_mo'), c('ffn2_norm'), c('ffn2_w1'), c('ffn2_w3'), c('ffn2_w2'),
                  c('final_norm')],
        out_specs=row(D_MODEL),
        out_shape=jax.ShapeDtypeStruct((n, D_MODEL), F32),
        compiler_params=pltpu.CompilerParams(dimension_semantics=("arbitrary",),
                                             vmem_limit_bytes=VMEM_LIMIT_BYTES),
        name="tail",
    )(x1, attn, lru, w_out_attn, wts['w_out_lru'], wts['xattn_norm'], wts['w_mq'], mk, mv, wts['w_mo'],
      wts['ffn2_norm'], wts['ffn2_w1'], wts['ffn2_w3'], wts['ffn2_w2'], wts['final_norm'])


def _memkv_kernel(mem_ref, g_ref, wmk_ref, wmv_ref, k_ref, v_ref, kb_ref, vb_ref):
    m = _rms(mem_ref[...], g_ref[...]).astype(BF16)
    k = _dot(m, wmk_ref[...])
    v = _dot(m, wmv_ref[...])
    k_ref[...] = k
    v_ref[...] = v
    kb_ref[...] = k.astype(BF16)
    vb_ref[...] = v.astype(BF16)


def _memkv_call(mem, wts):
    n = mem.shape[0]
    full = lambda dt: jax.ShapeDtypeStruct((n, D_MODEL), dt)
    spec = pl.BlockSpec((n, D_MODEL), lambda i: (0, 0))
    return pl.pallas_call(
        _memkv_kernel,
        grid=(1,),
        in_specs=[spec, _const_spec(wts['mem_norm'].shape), _const_spec(wts['w_mk'].shape),
                  _const_spec(wts['w_mv'].shape)],
        out_specs=[spec, spec, spec, spec],
        out_shape=[full(F32), full(F32), full(BF16), full(BF16)],
        compiler_params=pltpu.CompilerParams(dimension_semantics=("arbitrary",),
                                             vmem_limit_bytes=VMEM_LIMIT_BYTES),
        name="memkv",
    )(mem, wts['mem_norm'], wts['w_mk'], wts['w_mv'])


def _prep_weights(p):
    w = {}
    rowv = lambda a: a.reshape(1, -1).astype(F32)
    for name in ('ffn1_norm', 'mix_norm', 'q_norm', 'kv_norm', 'xattn_norm', 'ffn2_norm', 'final_norm',
                 'mem_norm', 'lru_out_norm', 'conv_b', 'lru_lambda'):
        w[name] = rowv(p[name])
    for name in ('ffn1_w1', 'ffn1_w3', 'ffn1_w2', 'ffn2_w1', 'ffn2_w3', 'ffn2_w2', 'w_mq', 'w_mk', 'w_mv',
                 'w_mo'):
        w[name] = p[name].astype(BF16)

    half = QK_ROPE // 2
    w_in = p['w_in']
    kpe_cols = w_in[:, ZC_KPE:ZC_KPE + QK_ROPE]
    pad_l = jnp.zeros((D_MODEL, QK_NOPE), F32)
    pad_r = jnp.zeros((D_MODEL, HEAD_PAD - QK_NOPE - QK_ROPE), F32)
    kpe_blk = jnp.concatenate([pad_l, kpe_cols, pad_r], axis=1)
    kpe_rot = jnp.concatenate([pad_l, kpe_cols[:, half:], kpe_cols[:, :half], pad_r], axis=1)
    lru_cols = w_in[:, Q_LORA + KV_LORA + QK_ROPE:]
    w['w_in_ext'] = jnp.concatenate([w_in[:, :Q_LORA + KV_LORA], kpe_blk, kpe_rot, lru_cols], axis=1).astype(BF16)

    wq = p['w_uq'].reshape(Q_LORA, MLA_HEADS, QK_NOPE + QK_ROPE)
    zq = lambda n: jnp.zeros((Q_LORA, MLA_HEADS, n), F32)
    q_main = jnp.concatenate([wq, zq(HEAD_PAD - QK_NOPE - QK_ROPE)], axis=2)
    q_rot = jnp.concatenate([zq(QK_NOPE), wq[:, :, QK_NOPE + half:], wq[:, :, QK_NOPE:QK_NOPE + half],
                             zq(HEAD_PAD - QK_NOPE - QK_ROPE)], axis=2)
    w['w_uq_ext'] = jnp.concatenate([q_main.reshape(Q_LORA, MLA_PAD), q_rot.reshape(Q_LORA, MLA_PAD)],
                                    axis=1).astype(BF16)

    wkv = p['w_ukv'].reshape(KV_LORA, MLA_HEADS, QK_NOPE + V_HEAD)
    zk = jnp.zeros((KV_LORA, MLA_HEADS, HEAD_PAD - QK_NOPE), F32)
    zv = jnp.zeros((KV_LORA, MLA_HEADS, HEAD_PAD - V_HEAD), F32)
    k_part = jnp.concatenate([wkv[:, :, :QK_NOPE], zk], axis=2).reshape(KV_LORA, MLA_PAD)
    v_part = jnp.concatenate([wkv[:, :, QK_NOPE:], zv], axis=2).reshape(KV_LORA, MLA_PAD)
    w['w_ukv_ext'] = jnp.concatenate([k_part, v_part], axis=1).astype(BF16)
    vt = jnp.concatenate([wkv[:, :, QK_NOPE:], jnp.zeros((KV_LORA, MLA_HEADS, VT_HEAD - V_HEAD), F32)], axis=2)
    w['w_uvt_ext'] = vt.reshape(KV_LORA, VT_ROWS).T.astype(BF16)

    inv_freq = ROPE_THETA ** (-jnp.arange(0, QK_ROPE, 2, dtype=F32) / QK_ROPE)
    zl = jnp.zeros((QK_NOPE,), F32)
    zr = jnp.zeros((HEAD_PAD - QK_NOPE - QK_ROPE,), F32)
    w['rope_freq'] = jnp.concatenate([zl, inv_freq, inv_freq, zr]).reshape(1, HEAD_PAD)
    w['rope_sign'] = jnp.concatenate([zl, -jnp.ones((half,), F32), jnp.ones((half,), F32), zr]).reshape(1, HEAD_PAD)

    g_attn = jnp.concatenate([p['attn_out_norm'].reshape(MLA_HEADS, V_HEAD),
                              jnp.zeros((MLA_HEADS, HEAD_PAD - V_HEAD), F32)], axis=1)
    w['attn_out_norm_pad'] = g_attn.reshape(1, MLA_PAD)
    w['attn_out_norm'] = rowv(p['attn_out_norm'])

    w_out = p['w_out']
    wo_attn = jnp.concatenate([w_out[:MLA_WIDTH].reshape(MLA_HEADS, V_HEAD, D_MODEL),
                               jnp.zeros((MLA_HEADS, HEAD_PAD - V_HEAD, D_MODEL), F32)], axis=1)
    w['w_out_attn_pad'] = wo_attn.reshape(MLA_PAD, D_MODEL).astype(BF16)
    w['w_out_attn'] = w_out[:MLA_WIDTH].astype(BF16)
    w['w_out_lru'] = w_out[MLA_WIDTH:].astype(BF16)

    def block_diag(wb):
        eye = jnp.eye(LRU_BLOCKS, dtype=F32)
        return (eye[:, None, :, None] * wb[:, :, None, :]).reshape(LRU_WIDTH, LRU_WIDTH)

    w['lru_wg'] = jnp.concatenate([block_diag(p['lru_wa']), block_diag(p['lru_wx'])], axis=1).astype(BF16)
    w['lru_bg'] = jnp.concatenate([p['lru_ba'].reshape(1, -1), p['lru_bx'].reshape(1, -1)], axis=1).astype(F32)
    w['conv_w'] = p['conv_w'].astype(F32)
    return w


def _pick_tile(n, cap):
    t = min(n, cap)
    while n % t:
        t //= 2
    return t


def _group(x, mk, mv, wts, *, past, conv_init, h_init):
    nbatch, seq, _ = x.shape
    n = nbatch * seq
    xf = x.reshape(n, D_MODEL)
    tm = _pick_tile(n, 512)
    ff_chunk = D_FF // 2
    pos_base = 0 if past is None else past[0].shape[1]
    x1, q, k, v, ckv, kpe, lru, cstate, hstate = _head_call(
        xf, conv_init, h_init, wts, seq_len=seq, pos_base=pos_base, tm=tm, ff_chunk=ff_chunk,
        v_transposed=past is None)
    if past is None:
        assert nbatch == 1
        tq = _pick_tile(n, 1024)
        attn = _mla_prompt_call(q, k, v, wts['attn_out_norm'], tq=tq, tk=tq)
    else:
        ckv_past, kpe_past = past
        kpe_pad = jnp.pad(kpe_past, ((0, 0), (0, 0), (QK_NOPE, HEAD_PAD - QK_NOPE - QK_ROPE)))
        attn = _mla_sample_call(q, k, v, ckv_past, kpe_pad, wts['w_ukv_ext'], wts['attn_out_norm_pad'],
                                t_new=seq)
    nb = tm // seq if seq < tm else 1
    y = _tail_call(x1, attn, lru, mk, mv, wts, seq_len=seq, tm=tm, nb=nb, ff_chunk=ff_chunk)
    return (y.reshape(nbatch, seq, D_MODEL), ckv.reshape(nbatch, seq, KV_LORA),
            kpe.reshape(nbatch, seq, QK_ROPE), cstate[:, SUBLANES - (CONV_W - 1):, :], hstate[:, 0, :])


def kernel(x_prompt, x_sample, mem_prompt, cache_mla_ckv, cache_mla_kpe, state_conv, state_lru, cache_mem_k, cache_mem_v, ffn1_norm, ffn1_w1, ffn1_w3, ffn1_w2, mix_norm, w_in, q_norm, w_uq, kv_norm, w_ukv, conv_w, conv_b, lru_wa, lru_ba, lru_wx, lru_bx, lru_lambda, attn_out_norm, lru_out_norm, w_out, mem_norm, xattn_norm, w_mq, w_mk, w_mv, w_mo, ffn2_norm, ffn2_w1, ffn2_w3, ffn2_w2, final_norm):
    depth = ffn1_norm.shape[0]
    assert depth == 1
    params = dict(ffn1_norm=ffn1_norm[0], ffn1_w1=ffn1_w1[0], ffn1_w3=ffn1_w3[0], ffn1_w2=ffn1_w2[0],
                  mix_norm=mix_norm[0], w_in=w_in[0], q_norm=q_norm[0], w_uq=w_uq[0], kv_norm=kv_norm[0],
                  w_ukv=w_ukv[0], conv_w=conv_w[0], conv_b=conv_b[0], lru_wa=lru_wa[0], lru_ba=lru_ba[0],
                  lru_wx=lru_wx[0], lru_bx=lru_bx[0], lru_lambda=lru_lambda[0],
                  attn_out_norm=attn_out_norm[0], lru_out_norm=lru_out_norm[0], w_out=w_out[0],
                  mem_norm=mem_norm[0], xattn_norm=xattn_norm[0], w_mq=w_mq[0], w_mk=w_mk[0], w_mv=w_mv[0],
                  w_mo=w_mo[0], ffn2_norm=ffn2_norm[0], ffn2_w1=ffn2_w1[0], ffn2_w3=ffn2_w3[0],
                  ffn2_w2=ffn2_w2[0], final_norm=final_norm)
    wts = _prep_weights(params)

    bp = x_prompt.shape[0]
    bs = x_sample.shape[0]
    n_mem = mem_prompt.shape[1]

    assert bp == 1
    mk_f, mv_f, mk_b, mv_b = _memkv_call(mem_prompt.reshape(bp * n_mem, D_MODEL), wts)
    zero_state = jnp.zeros((bp, SUBLANES, LRU_WIDTH), F32)
    y_p, ckv_p, kpe_p, conv_p, lru_p = _group(
        x_prompt, mk_b.reshape(bp, n_mem, D_MODEL), mv_b.reshape(bp, n_mem, D_MODEL), wts,
        past=None, conv_init=zero_state, h_init=zero_state)

    conv_init = jnp.pad(state_conv[0], ((0, 0), (SUBLANES - (CONV_W - 1), 0), (0, 0)))
    h_init = jnp.broadcast_to(state_lru[0][:, None, :], (bs, SUBLANES, LRU_WIDTH))
    y_s, ckv_s, kpe_s, conv_s, lru_s = _group(
        x_sample, cache_mem_k[0].reshape(bs, n_mem, D_MODEL).astype(BF16),
        cache_mem_v[0].reshape(bs, n_mem, D_MODEL).astype(BF16), wts,
        past=(cache_mla_ckv[0], cache_mla_kpe[0]), conv_init=conv_init, h_init=h_init)

    mem_shape = (1, bp, n_mem, MEM_HEADS, MEM_HEAD_DIM)
    return (y_p, y_s, ckv_p[None], kpe_p[None], conv_p[None], lru_p[None],
            mk_f.reshape(mem_shape), mv_f.reshape(mem_shape),
            ckv_s[None], kpe_s[None], conv_s[None], lru_s[None])
```

```python
import functools

import jax
import jax.numpy as jnp
from jax import lax
from jax.experimental import pallas as pl
from jax.experimental.pallas import tpu as pltpu

F32 = jnp.float32
BF16 = jnp.bfloat16

D_MODEL = 1024
CHUNK = 64
N_MEM = 256
MLA_HEADS = 8
Q_LORA = 384
KV_LORA = 256
QK_NOPE = 64
QK_ROPE = 32
V_HEAD = 64
MLA_WIDTH = MLA_HEADS * V_HEAD
MLA_SCALE = (QK_NOPE + QK_ROPE) ** -0.5
ROPE_THETA = 10000.0
LRU_WIDTH = 512
LRU_BLOCKS = 8
LRU_BLOCK = LRU_WIDTH // LRU_BLOCKS
CONV_W = 4
LRU_C = 8.0
MEM_HEADS = 4
MEM_HEAD_DIM = D_MODEL // MEM_HEADS
D_FF = 2816
EPS = 1e-6

LANES = 128
SUBLANES = 8
HEAD_PAD = LANES
MLA_PAD = MLA_HEADS * HEAD_PAD
VT_HEAD = V_HEAD + 16
VT_ROWS = MLA_HEADS * VT_HEAD
Q_SCALE = MLA_SCALE * 1.4426950408889634
QK_LOOKAHEAD = 2
ROW_STREAMS = 2
RESCALE_THRESHOLD = 64.0
FIRST_BLOCK_FLOOR = 64.0
ZC_Q = 0
ZC_KV = ZC_Q + Q_LORA
ZC_KPE = ZC_KV + KV_LORA
ZC_KPE_ROT = ZC_KPE + HEAD_PAD
ZC_XBR = ZC_KPE_ROT + HEAD_PAD
ZC_GBR = ZC_XBR + LRU_WIDTH
Z_COLS = ZC_GBR + LRU_WIDTH
NEG_BIG = -1e30
VMEM_LIMIT_BYTES = 56 * 1024 * 1024


def _const_spec(shape):
    nd = len(shape)
    return pl.BlockSpec(shape, lambda *_: (0,) * nd, pipeline_mode=pl.Buffered(1))


def _rms(x, g):
    ms = jnp.mean(x * x, axis=-1, keepdims=True)
    return x * lax.rsqrt(ms + EPS) * g


def _dot(a, b):
    return jnp.dot(a, b, preferred_element_type=F32)


def _dot_nt(a, b):
    return lax.dot_general(a, b, (((1,), (1,)), ((), ())), preferred_element_type=F32)


def _swiglu_half(xs, g_ref, w1_ref, w3_ref, w2_ref, ff_chunk):
    hs = [_rms(x, g_ref[...]).astype(BF16) for x in xs]
    accs = [jnp.zeros(x.shape, F32) for x in xs]
    for c in range(D_FF // ff_chunk):
        lo = c * ff_chunk
        ups = [(_dot(h, w1_ref[:, lo:lo + ff_chunk]), _dot(h, w3_ref[:, lo:lo + ff_chunk])) for h in hs]
        gs = [(a * jax.nn.sigmoid(a) * b).astype(BF16) for a, b in ups]
        accs = [acc + _dot(g, w2_ref[lo:lo + ff_chunk, :]) for acc, g in zip(accs, gs)]
    return [x + 0.5 * acc for x, acc in zip(xs, accs)]


def _row_streams(tm, n_streams):
    size = tm // n_streams
    return [(s * size, size) for s in range(n_streams)]


def _head_kernel(x_ref, cinit_ref, hinit_ref, g1_ref, w1_ref, w3_ref, w2_ref, gmix_ref, win_ref, gq_ref,
                 wuq_ref, gkv_ref, wukv_ref, wuvt_ref, freq_ref, sign_ref,
                 convw_ref, convb_ref, wg_ref, bg_ref, lam_ref, gl_ref,
                 x1_ref, q_ref, k_ref, v_ref, ckv_ref, kpe_ref, lru_ref, cstate_ref, hstate_ref,
                 xe_scr, ccar_scr, hcar_scr, zbuf_scr,
                 *, tm, seq_len, pos_base, ff_chunk, v_transposed, n_tiles, pipelined, n_streams):
    step = pl.program_id(0)
    lru_consts = (convw_ref, convb_ref, wg_ref, bg_ref, lam_ref, gl_ref)
    bcast = lambda hrow: jnp.broadcast_to(hrow, (SUBLANES, LRU_WIDTH))
    if pipelined:
        @pl.when(step == 0)
        def _():
            zbuf_scr[...] = jnp.zeros(zbuf_scr.shape, F32)
            ccar_scr[...] = jnp.zeros(ccar_scr.shape, F32)
            hcar_scr[...] = jnp.zeros(hcar_scr.shape, F32)

        new_seq = lax.rem(step - 1, seq_len // tm) == 0
        conv_prev = jnp.where(new_seq, cinit_ref[0], ccar_scr[...])
        h_prev = jnp.where(new_seq, hinit_ref[0, 0:1, :], hcar_scr[0:1, :])
        out, tail, h_last = _lru_block(zbuf_scr[:, :LRU_WIDTH], zbuf_scr[:, LRU_WIDTH:], conv_prev, h_prev,
                                       xe_scr, *lru_consts)
        lru_ref[...] = out
        ccar_scr[...] = tail
        hcar_scr[...] = bcast(h_last)
        cstate_ref[0] = tail
        hstate_ref[0] = bcast(h_last)
        tile = jnp.minimum(step, n_tiles - 1)
    else:
        tile = step

    streams = _row_streams(tm, n_streams)
    x1 = _swiglu_half([x_ref[lo:lo + sz, :] for lo, sz in streams], g1_ref, w1_ref, w3_ref, w2_ref, ff_chunk)
    x1 = x1[0] if n_streams == 1 else jnp.concatenate(x1, axis=0)
    x1_ref[...] = x1

    h = _rms(x1, gmix_ref[...]).astype(BF16)
    z = _dot(h, win_ref[...])

    row = lax.broadcasted_iota(jnp.int32, (tm, HEAD_PAD), 0) + tile * tm
    pos = (row & (seq_len - 1)) + pos_base
    ang = pos.astype(F32) * freq_ref[...]
    cos = jnp.cos(ang)
    sin = jnp.sin(ang) * sign_ref[...]

    cqn = _rms(z[:, ZC_Q:ZC_Q + Q_LORA], gq_ref[...]).astype(BF16)
    qz = _dot(cqn, wuq_ref[...])
    cos_q = cos * Q_SCALE
    sin_q = sin * Q_SCALE
    for hd in range(MLA_HEADS):
        lo = hd * HEAD_PAD
        qh = qz[:, lo:lo + HEAD_PAD] * cos_q + qz[:, MLA_PAD + lo:MLA_PAD + lo + HEAD_PAD] * sin_q
        q_ref[:, lo:lo + HEAD_PAD] = qh.astype(BF16)

    ckv = _rms(z[:, ZC_KV:ZC_KV + KV_LORA], gkv_ref[...])
    ckv_ref[...] = ckv
    kpe_blk = z[:, ZC_KPE:ZC_KPE + HEAD_PAD] * cos + z[:, ZC_KPE_ROT:ZC_KPE_ROT + HEAD_PAD] * sin
    kpe_ref[...] = kpe_blk[:, QK_NOPE:QK_NOPE + QK_ROPE]

    ckv_b = ckv.astype(BF16)
    kz = _dot(ckv_b, wukv_ref[:, :MLA_PAD])
    for hd in range(MLA_HEADS):
        lo = hd * HEAD_PAD
        k_ref[:, lo:lo + HEAD_PAD] = (kz[:, lo:lo + HEAD_PAD] + kpe_blk).astype(BF16)
    if v_transposed:
        vt = _dot_nt(wuvt_ref[...], ckv_b)
        ridx = lax.broadcasted_iota(jnp.int32, vt.shape, 0)
        is_one = ridx == V_HEAD
        for hd in range(1, MLA_HEADS):
            is_one = is_one | (ridx == hd * VT_HEAD + V_HEAD)
        v_ref[...] = jnp.where(is_one, 1.0, vt).astype(BF16)
    else:
        v_ref[...] = _dot(ckv_b, wukv_ref[:, MLA_PAD:]).astype(BF16)

    xbr = z[:, ZC_XBR:ZC_XBR + LRU_WIDTH]
    gbr = z[:, ZC_GBR:ZC_GBR + LRU_WIDTH]
    if pipelined:
        zbuf_scr[:, :LRU_WIDTH] = xbr
        zbuf_scr[:, LRU_WIDTH:] = gbr
    elif seq_len >= tm:
        @pl.when(lax.rem(step, seq_len // tm) == 0)
        def _():
            ccar_scr[...] = cinit_ref[0]
            hcar_scr[...] = hinit_ref[0]

        out, tail, h_last = _lru_block(xbr, gbr, ccar_scr[...], hcar_scr[0:1, :], xe_scr, *lru_consts)
        ccar_scr[...] = tail
        hcar_scr[...] = bcast(h_last)
        lru_ref[...] = out
        cstate_ref[0] = tail
        hstate_ref[0] = bcast(h_last)
    else:
        for b in range(tm // seq_len):
            lo = b * seq_len
            out, tail, h_last = _lru_block(xbr[lo:lo + seq_len, :], gbr[lo:lo + seq_len, :], cinit_ref[b],
                                           hinit_ref[b, 0:1, :], xe_scr, *lru_consts)
            lru_ref[lo:lo + seq_len, :] = out
            cstate_ref[b] = tail
            hstate_ref[b] = bcast(h_last)


def _head_call(x, conv_init, h_init, wts, *, seq_len, pos_base, tm, ff_chunk, v_transposed):
    n = x.shape[0]
    nseq = n // seq_len
    assert n % tm == 0 and seq_len & (seq_len - 1) == 0 and seq_len >= SUBLANES
    assert seq_len % tm == 0 or tm % seq_len == 0
    nsub = max(1, tm // seq_len)
    n_tiles = n // tm
    pipelined = seq_len > tm
    if pipelined:
        row = lambda w: pl.BlockSpec((tm, w), lambda i: (jnp.minimum(i, n_tiles - 1), 0))
        lru_tile = lambda i: jnp.maximum(i - 1, 0)
        state = pl.BlockSpec((1, SUBLANES, LRU_WIDTH), lambda i: ((lru_tile(i) * tm) // seq_len, 0, 0))
    else:
        row = lambda w: pl.BlockSpec((tm, w), lambda i: (i, 0))
        state = pl.BlockSpec((nsub, SUBLANES, LRU_WIDTH), lambda i: ((i * tm) // (seq_len * nsub), 0, 0))
    consts = [wts['ffn1_norm'], wts['ffn1_w1'], wts['ffn1_w3'], wts['ffn1_w2'], wts['mix_norm'],
              wts['w_in_ext'], wts['q_norm'], wts['w_uq_ext'], wts['kv_norm'], wts['w_ukv_ext'],
              wts['w_uvt_ext'], wts['rope_freq'], wts['rope_sign'],
              wts['conv_w'], wts['conv_b'], wts['lru_wg'], wts['lru_bg'], wts['lru_lambda'],
              wts['lru_out_norm']]
    v_shape = (VT_ROWS, n) if v_transposed else (n, MLA_PAD)
    if v_transposed:
        v_spec = pl.BlockSpec((VT_ROWS, tm), lambda i: (0, jnp.minimum(i, n_tiles - 1)))
    else:
        v_spec = row(MLA_PAD)
    state_shape = jax.ShapeDtypeStruct((nseq, SUBLANES, LRU_WIDTH), F32)
    out_shapes = [
        jax.ShapeDtypeStruct((n, D_MODEL), F32),
        jax.ShapeDtypeStruct((n, MLA_PAD), BF16),
        jax.ShapeDtypeStruct((n, MLA_PAD), BF16),
        jax.ShapeDtypeStruct(v_shape, BF16),
        jax.ShapeDtypeStruct((n, KV_LORA), F32),
        jax.ShapeDtypeStruct((n, QK_ROPE), F32),
        jax.ShapeDtypeStruct((n, LRU_WIDTH), BF16),
        state_shape,
        state_shape,
    ]
    out_specs = [row(s.shape[1]) for s in out_shapes[:7]] + [state, state]
    out_specs[3] = v_spec
    if pipelined:
        out_specs[6] = pl.BlockSpec((tm, LRU_WIDTH), lambda i: (lru_tile(i), 0))
    lru_rows = min(tm, seq_len)
    return pl.pallas_call(
        functools.partial(_head_kernel, tm=tm, seq_len=seq_len, pos_base=pos_base, ff_chunk=ff_chunk,
                          v_transposed=v_transposed, n_tiles=n_tiles, pipelined=pipelined,
                          n_streams=ROW_STREAMS if tm % (ROW_STREAMS * 16) == 0 and tm >= 512 else 1),
        grid=(n_tiles + 1 if pipelined else n_tiles,),
        in_specs=[row(D_MODEL), state, state] + [_const_spec(c.shape) for c in consts],
        out_specs=out_specs,
        out_shape=out_shapes,
        scratch_shapes=[pltpu.VMEM((lru_rows + SUBLANES, LRU_WIDTH), F32),
                        pltpu.VMEM((SUBLANES, LRU_WIDTH), F32),
                        pltpu.VMEM((SUBLANES, LRU_WIDTH), F32),
                        pltpu.VMEM((tm if pipelined else SUBLANES, 2 * LRU_WIDTH), F32)],
        compiler_params=pltpu.CompilerParams(dimension_semantics=("arbitrary",),
                                             vmem_limit_bytes=VMEM_LIMIT_BYTES),
        name="head",
    )(x, conv_init, h_init, *consts)


def _attn_finalize(o_heads, gattn_ref, out_ref):
    ss = o_heads[0] * o_heads[0]
    for o in o_heads[1:]:
        ss = ss + o * o
    ms = jnp.sum(ss, axis=-1, keepdims=True) * (1.0 / MLA_WIDTH)
    inv = lax.rsqrt(ms + EPS)
    for hd, o in enumerate(o_heads):
        lo = hd * HEAD_PAD
        out_ref[:, lo:lo + HEAD_PAD] = (o * inv * gattn_ref[:, lo:lo + HEAD_PAD]).astype(BF16)


def _mla_prompt_step(q_ref, k_ref, vt_ref, m_scr, mx_scr, acc_scr, cur, first, visible, *, rescale,
                     keys=None, queries=None, second_part=False):
    k0, k1 = keys if keys is not None else (0, k_ref.shape[0])
    q0, q1 = queries if queries is not None else (0, q_ref.shape[0])

    def scores(hd):
        lo = hd * HEAD_PAD
        return _dot_nt(k_ref[k0:k1, lo:lo + HEAD_PAD], q_ref[q0:q1, lo:lo + HEAD_PAD])

    pending = [scores(hd) for hd in range(QK_LOOKAHEAD)]
    for hd in range(MLA_HEADS):
        st = pending[hd]
        if hd + QK_LOOKAHEAD < MLA_HEADS:
            pending.append(scores(hd + QK_LOOKAHEAD))
        if visible is not None:
            st = jnp.where(visible, st, NEG_BIG)
        vo = hd * VT_HEAD
        blk_max = jnp.max(st, axis=0, keepdims=True)
        if rescale:
            m_prev = jnp.where(first, NEG_BIG, m_scr[hd:hd + 1, q0:q1])
            m_new = jnp.maximum(m_prev, blk_max)
            alpha = jnp.exp2(m_prev - m_new)
            pt = jnp.exp2(st - m_new).astype(BF16)
            acc_scr[cur, vo:vo + VT_HEAD, q0:q1] = (alpha * acc_scr[cur, vo:vo + VT_HEAD, q0:q1]
                                                    + _dot(vt_ref[vo:vo + VT_HEAD, k0:k1], pt))
            m_scr[hd:hd + 1, q0:q1] = m_new
        else:
            src = 1 - cur if second_part else cur
            mx_scr[hd:hd + 1, q0:q1] = jnp.maximum(mx_scr[hd:hd + 1, q0:q1], blk_max) if second_part else blk_max
            pt = jnp.exp2(st - m_scr[hd:hd + 1, q0:q1]).astype(BF16)
            acc_scr[1 - cur, vo:vo + VT_HEAD, q0:q1] = (acc_scr[src, vo:vo + VT_HEAD, q0:q1]
                                                        + _dot(vt_ref[vo:vo + VT_HEAD, k0:k1], pt))


def _mla_prompt_kernel(qi_ref, ki_ref, q_ref, k_ref, vt_ref, gattn_ref, out_ref,
                       m_scr, mx_scr, acc_scr, cur_scr, *, tq, tk):
    step = pl.program_id(0)
    q_lo = qi_ref[step] * tq
    k_lo = ki_ref[step] * tk
    first = k_lo == 0

    @pl.when(first)
    def _():
        m_scr[...] = jnp.zeros(m_scr.shape, F32)
        acc_scr[0] = jnp.zeros(acc_scr.shape[1:], F32)
        cur_scr[0] = 0

    cur = cur_scr[0]
    fully_visible = k_lo + tk <= q_lo + CHUNK

    def chunk_mask():
        kc = (lax.broadcasted_iota(jnp.int32, (tk, tq), 0) + k_lo) // CHUNK
        qc = (lax.broadcasted_iota(jnp.int32, (tk, tq), 1) + q_lo) // CHUNK
        return kc <= qc

    step_fn = functools.partial(_mla_prompt_step, q_ref, k_ref, vt_ref, m_scr, mx_scr, acc_scr, cur, first)

    @pl.when(fully_visible)
    def _():
        step_fn(None, rescale=False)

    @pl.when(jnp.logical_not(fully_visible))
    def _():
        if tq == tk:
            vis = chunk_mask()
            step_fn(vis[:tk // 2, :], rescale=False, keys=(0, tk // 2))
            step_fn(vis[tk // 2:, tq // 2:], rescale=False, keys=(tk // 2, tk), queries=(tq // 2, tq),
                    second_part=True)
        else:
            step_fn(chunk_mask(), rescale=False)

    excess = mx_scr[...] - m_scr[...]
    redo = (jnp.max(excess) > RESCALE_THRESHOLD) | (first & (jnp.min(excess) < -FIRST_BLOCK_FLOOR))

    @pl.when(redo)
    def _():
        step_fn(chunk_mask(), rescale=True)

    @pl.when(jnp.logical_not(redo))
    def _():
        cur_scr[0] = 1 - cur

    @pl.when(k_lo + tk >= q_lo + tq)
    def _():
        fin = cur_scr[0]
        parts = []
        for hd in range(MLA_HEADS):
            vo = hd * VT_HEAD
            parts.append(acc_scr[fin, vo:vo + V_HEAD, :] / acc_scr[fin, vo + V_HEAD:vo + V_HEAD + 1, :])
        o = jnp.concatenate(parts, axis=0).T
        out_ref[...] = _rms(o, gattn_ref[...]).astype(BF16)


def _mla_prompt_call(q, k, vt, gattn, *, tq, tk):
    n = q.shape[0]
    assert n % tq == 0 and tq % tk == 0 and tk % CHUNK == 0
    pairs = [(qi, ki) for qi in range(n // tq) for ki in range((qi + 1) * tq // tk)]
    qi_tab = jnp.asarray([p[0] for p in pairs], jnp.int32)
    ki_tab = jnp.asarray([p[1] for p in pairs], jnp.int32)
    grid_spec = pltpu.PrefetchScalarGridSpec(
        num_scalar_prefetch=2,
        grid=(len(pairs),),
        in_specs=[pl.BlockSpec((tq, MLA_PAD), lambda s, qi, ki: (qi[s], 0)),
                  pl.BlockSpec((tk, MLA_PAD), lambda s, qi, ki: (ki[s], 0)),
                  pl.BlockSpec((VT_ROWS, tk), lambda s, qi, ki: (0, ki[s])),
                  pl.BlockSpec(gattn.shape, lambda s, qi, ki: (0, 0))],
        out_specs=pl.BlockSpec((tq, MLA_WIDTH), lambda s, qi, ki: (qi[s], 0)),
        scratch_shapes=[pltpu.VMEM((MLA_HEADS, tq), F32),
                        pltpu.VMEM((MLA_HEADS, tq), F32),
                        pltpu.VMEM((2, VT_ROWS, tq), F32),
                        pltpu.SMEM((1,), jnp.int32)],
    )
    return pl.pallas_call(
        functools.partial(_mla_prompt_kernel, tq=tq, tk=tk),
        grid_spec=grid_spec,
        out_shape=jax.ShapeDtypeStruct((n, MLA_WIDTH), BF16),
        compiler_params=pltpu.CompilerParams(dimension_semantics=("arbitrary",),
                                             vmem_limit_bytes=VMEM_LIMIT_BYTES),
        name="mla_prompt",
    )(qi_tab, ki_tab, q, k, vt, gattn)


def _mla_sample_kernel(q_ref, kn_ref, vn_ref, ckv_ref, kpe_ref, wukv_ref, gattn_ref, out_ref):
    t = q_ref.shape[0]
    heads = [slice(hd * HEAD_PAD, (hd + 1) * HEAD_PAD) for hd in range(MLA_HEADS)]
    cp = ckv_ref[0].astype(BF16)
    past = cp.shape[0]
    kpe = jnp.concatenate([jnp.zeros((past, QK_NOPE), F32), kpe_ref[0],
                           jnp.zeros((past, HEAD_PAD - QK_NOPE - QK_ROPE), F32)], axis=1).astype(BF16)
    q_all = jnp.concatenate([q_ref[:, hs] for hs in heads], axis=0)
    q_lat = jnp.concatenate([_dot_nt(q_ref[:, hs], wukv_ref[:, hs]) for hs in heads], axis=0)
    s1 = _dot_nt(q_lat.astype(BF16), cp) + _dot_nt(q_all, kpe)
    s2 = jnp.concatenate([_dot_nt(q_ref[:, hs], kn_ref[:, hs]) for hs in heads], axis=0)
    m = jnp.maximum(jnp.max(s1, axis=-1, keepdims=True), jnp.max(s2, axis=-1, keepdims=True))
    p1 = jnp.exp2(s1 - m)
    p2 = jnp.exp2(s2 - m)
    l = jnp.sum(p1, axis=-1, keepdims=True) + jnp.sum(p2, axis=-1, keepdims=True)
    p2 = p2.astype(BF16)
    o_lat = _dot(p1.astype(BF16), cp).astype(BF16)
    o_heads = []
    for hd, hs in enumerate(heads):
        rows = slice(hd * t, (hd + 1) * t)
        w_uv = wukv_ref[:, MLA_PAD + hd * HEAD_PAD:MLA_PAD + (hd + 1) * HEAD_PAD]
        o = _dot(o_lat[rows, :], w_uv) + _dot(p2[rows, :], vn_ref[:, hs])
        o_heads.append(o / l[rows, :])
    _attn_finalize(o_heads, gattn_ref, out_ref)


def _mla_sample_call(q, k, v, ckv_past, kpe_past, w_ukv_ext, gattn, *, t_new):
    n = q.shape[0]
    nbatch, past, _ = ckv_past.shape
    assert n == nbatch * t_new
    rows = pl.BlockSpec((t_new, MLA_PAD), lambda b: (b, 0))
    return pl.pallas_call(
        _mla_sample_kernel,
        grid=(nbatch,),
        in_specs=[rows, rows, rows,
                  pl.BlockSpec((1, past, KV_LORA), lambda b: (b, 0, 0)),
                  pl.BlockSpec((1, past, QK_ROPE), lambda b: (b, 0, 0)),
                  _const_spec(w_ukv_ext.shape), _const_spec(gattn.shape)],
        out_specs=rows,
        out_shape=jax.ShapeDtypeStruct((n, MLA_PAD), BF16),
        compiler_params=pltpu.CompilerParams(dimension_semantics=("arbitrary",),
                                             vmem_limit_bytes=VMEM_LIMIT_BYTES),
        name="mla_sample",
    )(q, k, v, ckv_past, kpe_past, w_ukv_ext, gattn)


def _lru_block(x, gate, conv_prev, h_prev, xe_scr, convw_ref, convb_ref, wg_ref, bg_ref, lam_ref, gl_ref):
    tm = x.shape[0]
    xe_scr[0:SUBLANES, :] = conv_prev
    xe_scr[SUBLANES:SUBLANES + tm, :] = x
    xc = convb_ref[...] + xe_scr[pl.ds(SUBLANES - 3, tm), :] * convw_ref[0:1, :]
    xc = xc + xe_scr[pl.ds(SUBLANES - 2, tm), :] * convw_ref[1:2, :]
    xc = xc + xe_scr[pl.ds(SUBLANES - 1, tm), :] * convw_ref[2:3, :]
    xc = xc + x * convw_ref[3:4, :]
    tail = x[tm - SUBLANES:, :]

    y = _dot(xc.astype(BF16), wg_ref[...]) + bg_ref[...]
    r = jax.nn.sigmoid(y[:, :LRU_WIDTH])
    ig = jax.nn.sigmoid(y[:, LRU_WIDTH:])
    nl = -lam_ref[...]
    softplus = jnp.maximum(nl, 0.0) + jnp.log1p(jnp.exp(-jnp.abs(nl)))
    log_a = -LRU_C * r * softplus
    a = jnp.exp(log_a)
    th = jnp.tanh(log_a)
    one_minus_a2 = -2.0 * th / (1.0 - th)
    b = jnp.sqrt(one_minus_a2) * ig * xc

    sub = lax.broadcasted_iota(jnp.int32, (tm, LRU_WIDTH), 0) & (SUBLANES - 1)
    s = 1
    while s < SUBLANES:
        valid = sub >= s
        a_sh = pltpu.roll(a, s, axis=0)
        b_sh = pltpu.roll(b, s, axis=0)
        b = jnp.where(valid, a * b_sh + b, b)
        a = jnp.where(valid, a * a_sh, a)
        s *= 2
    h_groups = []
    for g in range(tm // SUBLANES):
        lo = g * SUBLANES
        hg = b[lo:lo + SUBLANES, :] + a[lo:lo + SUBLANES, :] * h_prev
        h_groups.append(hg)
        h_prev = hg[SUBLANES - 1:SUBLANES, :]
    h = jnp.concatenate(h_groups, axis=0)
    out = _rms(jax.nn.gelu(gate) * h, gl_ref[...]).astype(BF16)
    return out, tail, h_prev


def _tail_kernel(x1_ref, attn_ref, lru_ref, woa_ref, wol_ref, gx_ref, wmq_ref, mk_ref, mv_ref, wmo_ref,
                 g2_ref, w1_ref, w3_ref, w2_ref, gf_ref, y_ref, *, tm, nb, ff_chunk, n_streams):
    streams = _row_streams(tm, n_streams)
    x2s = [x1_ref[lo:lo + sz, :] + _dot(attn_ref[lo:lo + sz, :], woa_ref[...])
           + _dot(lru_ref[lo:lo + sz, :], wol_ref[...]) for lo, sz in streams]

    hqs = [_rms(x2, gx_ref[...]).astype(BF16) for x2 in x2s]
    qs = [(_dot(hq, wmq_ref[...]) * (MEM_HEAD_DIM ** -0.5)).astype(BF16) for hq in hqs]

    def cross_attend(q, mem_batch):
        o_heads = []
        for hd in range(MEM_HEADS):
            lo = hd * MEM_HEAD_DIM
            s = _dot_nt(q[:, lo:lo + MEM_HEAD_DIM], mk_ref[mem_batch, :, lo:lo + MEM_HEAD_DIM])
            p = jnp.exp(s - jnp.max(s, axis=-1, keepdims=True))
            l = jnp.sum(p, axis=-1, keepdims=True)
            o_heads.append(_dot(p.astype(BF16), mv_ref[mem_batch, :, lo:lo + MEM_HEAD_DIM]) / l)
        return jnp.concatenate(o_heads, axis=1)

    if nb == 1:
        os_ = [cross_attend(q, 0) for q in qs]
    else:
        rows = tm // nb
        os_ = [jnp.concatenate([cross_attend(qs[0][b * rows:(b + 1) * rows, :], b) for b in range(nb)], axis=0)]
    x3s = [x2 + _dot(o.astype(BF16), wmo_ref[...]) for x2, o in zip(x2s, os_)]

    x4s = _swiglu_half(x3s, g2_ref, w1_ref, w3_ref, w2_ref, ff_chunk)
    for (lo, sz), x4 in zip(streams, x4s):
        y_ref[lo:lo + sz, :] = _rms(x4, gf_ref[...])


def _tail_call(x1, attn, lru, mk, mv, wts, *, seq_len, tm, nb, ff_chunk):
    n = x1.shape[0]
    assert n % tm == 0 and tm % nb == 0 and mk.shape[0] * seq_len == n
    assert (nb == 1 and seq_len % tm == 0) or nb * seq_len == tm
    row = lambda w: pl.BlockSpec((tm, w), lambda i: (i, 0))
    mem = pl.BlockSpec((nb, N_MEM, D_MODEL), lambda i: ((i * tm) // (seq_len * nb), 0, 0))
    c = lambda name: _const_spec(wts[name].shape)
    w_out_attn = wts['w_out_attn'] if attn.shape[1] == MLA_WIDTH else wts['w_out_attn_pad']
    return pl.pallas_call(
        functools.partial(_tail_kernel, tm=tm, nb=nb, ff_chunk=ff_chunk,
                          n_streams=ROW_STREAMS if nb == 1 and tm % (ROW_STREAMS * 16) == 0 else 1),
        grid=(n // tm,),
        in_specs=[row(D_MODEL), row(attn.shape[1]), row(LRU_WIDTH), _const_spec(w_out_attn.shape),
                  c('w_out_lru'), c('xattn_norm'), c('w_mq'),
                  mem, mem, c('w_mo'), c('ffn2_norm'), c('ffn2_w1'), c('ffn2_w3'), c('ffn2_w2'),
                  c('final_norm')],
        out_specs=row(D_MODEL),
        out_shape=jax.ShapeDtypeStruct((n, D_MODEL), F32),
        compiler_params=pltpu.CompilerParams(dimension_semantics=("arbitrary",),
                                             vmem_limit_bytes=VMEM_LIMIT_BYTES),
        name="tail",
    )(x1, attn, lru, w_out_attn, wts['w_out_lru'], wts['xattn_norm'], wts['w_mq'], mk, mv, wts['w_mo'],
      wts['ffn2_norm'], wts['ffn2_w1'], wts['ffn2_w3'], wts['ffn2_w2'], wts['final_norm'])


def _memkv_kernel(mem_ref, g_ref, wmk_ref, wmv_ref, k_ref, v_ref, kb_ref, vb_ref):
    m = _rms(mem_ref[...], g_ref[...]).astype(BF16)
    k = _dot(m, wmk_ref[...])
    v = _dot(m, wmv_ref[...])
    k_ref[...] = k
    v_ref[...] = v
    kb_ref[...] = k.astype(BF16)
    vb_ref[...] = v.astype(BF16)


def _memkv_call(mem, wts):
    n = mem.shape[0]
    full = lambda dt: jax.ShapeDtypeStruct((n, D_MODEL), dt)
    spec = pl.BlockSpec((n, D_MODEL), lambda i: (0, 0))
    return pl.pallas_call(
        _memkv_kernel,
        grid=(1,),
        in_specs=[spec, _const_spec(wts['mem_norm'].shape), _const_spec(wts['w_mk'].shape),
                  _const_spec(wts['w_mv'].shape)],
        out_specs=[spec, spec, spec, spec],
        out_shape=[full(F32), full(F32), full(BF16), full(BF16)],
        compiler_params=pltpu.CompilerParams(dimension_semantics=("arbitrary",),
                                             vmem_limit_bytes=VMEM_LIMIT_BYTES),
        name="memkv",
    )(mem, wts['mem_norm'], wts['w_mk'], wts['w_mv'])


def _cast_kernel(w_ref, o_ref):
    o_ref[...] = w_ref[...].astype(BF16)


def _cast_bf16_call(w):
    rows, cols = w.shape
    blk = rows // 4
    assert rows % 4 == 0 and blk % 16 == 0 and cols % LANES == 0
    spec = pl.BlockSpec((blk, cols), lambda i: (i, 0))
    return pl.pallas_call(
        _cast_kernel,
        grid=(4,),
        in_specs=[spec],
        out_specs=spec,
        out_shape=jax.ShapeDtypeStruct((rows, cols), BF16),
        compiler_params=pltpu.CompilerParams(dimension_semantics=("arbitrary",),
                                             vmem_limit_bytes=VMEM_LIMIT_BYTES),
        name="cast_bf16",
    )(w)


def _prep_weights(p):
    w = {}
    rowv = lambda a: a.reshape(1, -1).astype(F32)
    for name in ('ffn1_norm', 'mix_norm', 'q_norm', 'kv_norm', 'xattn_norm', 'ffn2_norm', 'final_norm',
                 'mem_norm', 'lru_out_norm', 'conv_b', 'lru_lambda'):
        w[name] = rowv(p[name])
    for name in ('ffn1_w1', 'ffn1_w3', 'ffn1_w2', 'ffn2_w1', 'ffn2_w3', 'ffn2_w2', 'w_mq', 'w_mk', 'w_mv',
                 'w_mo'):
        w[name] = _cast_bf16_call(p[name])

    half = QK_ROPE // 2
    w_in = p['w_in']
    kpe_cols = w_in[:, ZC_KPE:ZC_KPE + QK_ROPE]
    pad_l = jnp.zeros((D_MODEL, QK_NOPE), F32)
    pad_r = jnp.zeros((D_MODEL, HEAD_PAD - QK_NOPE - QK_ROPE), F32)
    kpe_blk = jnp.concatenate([pad_l, kpe_cols, pad_r], axis=1)
    kpe_rot = jnp.concatenate([pad_l, kpe_cols[:, half:], kpe_cols[:, :half], pad_r], axis=1)
    lru_cols = w_in[:, Q_LORA + KV_LORA + QK_ROPE:]
    w['w_in_ext'] = jnp.concatenate([w_in[:, :Q_LORA + KV_LORA], kpe_blk, kpe_rot, lru_cols], axis=1).astype(BF16)

    wq = p['w_uq'].reshape(Q_LORA, MLA_HEADS, QK_NOPE + QK_ROPE)
    zq = lambda n: jnp.zeros((Q_LORA, MLA_HEADS, n), F32)
    q_main = jnp.concatenate([wq, zq(HEAD_PAD - QK_NOPE - QK_ROPE)], axis=2)
    q_rot = jnp.concatenate([zq(QK_NOPE), wq[:, :, QK_NOPE + half:], wq[:, :, QK_NOPE:QK_NOPE + half],
                             zq(HEAD_PAD - QK_NOPE - QK_ROPE)], axis=2)
    w['w_uq_ext'] = jnp.concatenate([q_main.reshape(Q_LORA, MLA_PAD), q_rot.reshape(Q_LORA, MLA_PAD)],
                                    axis=1).astype(BF16)

    wkv = p['w_ukv'].reshape(KV_LORA, MLA_HEADS, QK_NOPE + V_HEAD)
    zk = jnp.zeros((KV_LORA, MLA_HEADS, HEAD_PAD - QK_NOPE), F32)
    zv = jnp.zeros((KV_LORA, MLA_HEADS, HEAD_PAD - V_HEAD), F32)
    k_part = jnp.concatenate([wkv[:, :, :QK_NOPE], zk], axis=2).reshape(KV_LORA, MLA_PAD)
    v_part = jnp.concatenate([wkv[:, :, QK_NOPE:], zv], axis=2).reshape(KV_LORA, MLA_PAD)
    w['w_ukv_ext'] = jnp.concatenate([k_part, v_part], axis=1).astype(BF16)
    vt = jnp.concatenate([wkv[:, :, QK_NOPE:], jnp.zeros((KV_LORA, MLA_HEADS, VT_HEAD - V_HEAD), F32)], axis=2)
    w['w_uvt_ext'] = vt.reshape(KV_LORA, VT_ROWS).T.astype(BF16)

    inv_freq = ROPE_THETA ** (-jnp.arange(0, QK_ROPE, 2, dtype=F32) / QK_ROPE)
    zl = jnp.zeros((QK_NOPE,), F32)
    zr = jnp.zeros((HEAD_PAD - QK_NOPE - QK_ROPE,), F32)
    w['rope_freq'] = jnp.concatenate([zl, inv_freq, inv_freq, zr]).reshape(1, HEAD_PAD)
    w['rope_sign'] = jnp.concatenate([zl, -jnp.ones((half,), F32), jnp.ones((half,), F32), zr]).reshape(1, HEAD_PAD)

    g_attn = jnp.concatenate([p['attn_out_norm'].reshape(MLA_HEADS, V_HEAD),
                              jnp.zeros((MLA_HEADS, HEAD_PAD - V_HEAD), F32)], axis=1)
    w['attn_out_norm_pad'] = g_attn.reshape(1, MLA_PAD)
    w['attn_out_norm'] = rowv(p['attn_out_norm'])

    w_out = p['w_out']
    wo_attn = jnp.concatenate([w_out[:MLA_WIDTH].reshape(MLA_HEADS, V_HEAD, D_MODEL),
                               jnp.zeros((MLA_HEADS, HEAD_PAD - V_HEAD, D_MODEL), F32)], axis=1)
    w['w_out_attn_pad'] = wo_attn.reshape(MLA_PAD, D_MODEL).astype(BF16)
    w['w_out_attn'] = w_out[:MLA_WIDTH].astype(BF16)
    w['w_out_lru'] = w_out[MLA_WIDTH:].astype(BF16)

    def block_diag(wb):
        eye = jnp.eye(LRU_BLOCKS, dtype=F32)
        return (eye[:, None, :, None] * wb[:, :, None, :]).reshape(LRU_WIDTH, LRU_WIDTH)

    w['lru_wg'] = jnp.concatenate([block_diag(p['lru_wa']), block_diag(p['lru_wx'])], axis=1).astype(BF16)
    w['lru_bg'] = jnp.concatenate([p['lru_ba'].reshape(1, -1), p['lru_bx'].reshape(1, -1)], axis=1).astype(F32)
    w['conv_w'] = p['conv_w'].astype(F32)
    return w


def _pick_tile(n, cap):
    t = min(n, cap)
    while n % t:
        t //= 2
    return t


def _group(x, mk, mv, wts, *, past, conv_init, h_init):
    nbatch, seq, _ = x.shape
    n = nbatch * seq
    xf = x.reshape(n, D_MODEL)
    tm = _pick_tile(n, 512)
    ff_chunk = D_FF // 2
    pos_base = 0 if past is None else past[0].shape[1]
    x1, q, k, v, ckv, kpe, lru, cstate, hstate = _head_call(
        xf, conv_init, h_init, wts, seq_len=seq, pos_base=pos_base, tm=tm, ff_chunk=ff_chunk,
        v_transposed=past is None)
    if past is None:
        assert nbatch == 1
        tq = _pick_tile(n, 1024)
        attn = _mla_prompt_call(q, k, v, wts['attn_out_norm'], tq=tq, tk=tq)
    else:
        ckv_past, kpe_past = past
        attn = _mla_sample_call(q, k, v, ckv_past, kpe_past, wts['w_ukv_ext'], wts['attn_out_norm_pad'],
                                t_new=seq)
    nb = tm // seq if seq < tm else 1
    y = _tail_call(x1, attn, lru, mk, mv, wts, seq_len=seq, tm=tm, nb=nb, ff_chunk=ff_chunk)
    return (y.reshape(nbatch, seq, D_MODEL), ckv.reshape(nbatch, seq, KV_LORA),
            kpe.reshape(nbatch, seq, QK_ROPE), cstate[:, SUBLANES - (CONV_W - 1):, :], hstate[:, 0, :])


def kernel(x_prompt, x_sample, mem_prompt, cache_mla_ckv, cache_mla_kpe, state_conv, state_lru, cache_mem_k, cache_mem_v, ffn1_norm, ffn1_w1, ffn1_w3, ffn1_w2, mix_norm, w_in, q_norm, w_uq, kv_norm, w_ukv, conv_w, conv_b, lru_wa, lru_ba, lru_wx, lru_bx, lru_lambda, attn_out_norm, lru_out_norm, w_out, mem_norm, xattn_norm, w_mq, w_mk, w_mv, w_mo, ffn2_norm, ffn2_w1, ffn2_w3, ffn2_w2, final_norm):
    depth = ffn1_norm.shape[0]
    assert depth == 1
    params = dict(ffn1_norm=ffn1_norm[0], ffn1_w1=ffn1_w1[0], ffn1_w3=ffn1_w3[0], ffn1_w2=ffn1_w2[0],
                  mix_norm=mix_norm[0], w_in=w_in[0], q_norm=q_norm[0], w_uq=w_uq[0], kv_norm=kv_norm[0],
                  w_ukv=w_ukv[0], conv_w=conv_w[0], conv_b=conv_b[0], lru_wa=lru_wa[0], lru_ba=lru_ba[0],
                  lru_wx=lru_wx[0], lru_bx=lru_bx[0], lru_lambda=lru_lambda[0],
                  attn_out_norm=attn_out_norm[0], lru_out_norm=lru_out_norm[0], w_out=w_out[0],
                  mem_norm=mem_norm[0], xattn_norm=xattn_norm[0], w_mq=w_mq[0], w_mk=w_mk[0], w_mv=w_mv[0],
                  w_mo=w_mo[0], ffn2_norm=ffn2_norm[0], ffn2_w1=ffn2_w1[0], ffn2_w3=ffn2_w3[0],
                  ffn2_w2=ffn2_w2[0], final_norm=final_norm)
    wts = _prep_weights(params)

    bp = x_prompt.shape[0]
    bs = x_sample.shape[0]
    n_mem = mem_prompt.shape[1]

    assert bp == 1
    mk_f, mv_f, mk_b, mv_b = _memkv_call(mem_prompt.reshape(bp * n_mem, D_MODEL), wts)
    zero_state = jnp.zeros((bp, SUBLANES, LRU_WIDTH), F32)
    y_p, ckv_p, kpe_p, conv_p, lru_p = _group(
        x_prompt, mk_b.reshape(bp, n_mem, D_MODEL), mv_b.reshape(bp, n_mem, D_MODEL), wts,
        past=None, conv_init=zero_state, h_init=zero_state)

    conv_init = jnp.pad(state_conv[0], ((0, 0), (SUBLANES - (CONV_W - 1), 0), (0, 0)))
    h_init = jnp.broadcast_to(state_lru[0][:, None, :], (bs, SUBLANES, LRU_WIDTH))
    y_s, ckv_s, kpe_s, conv_s, lru_s = _group(
        x_sample, cache_mem_k[0].reshape(bs, n_mem, D_MODEL).astype(BF16),
        cache_mem_v[0].reshape(bs, n_mem, D_MODEL).astype(BF16), wts,
        past=(cache_mla_ckv[0], cache_mla_kpe[0]), conv_init=conv_init, h_init=h_init)

    mem_shape = (1, bp, n_mem, MEM_HEADS, MEM_HEAD_DIM)
    return (y_p, y_s, ckv_p[None], kpe_p[None], conv_p[None], lru_p[None],
            mk_f.reshape(mem_shape), mv_f.reshape(mem_shape),
            ckv_s[None], kpe_s[None], conv_s[None], lru_s[None])
```

```python
import functools

import jax
import jax.numpy as jnp
from jax import lax
from jax.experimental import pallas as pl
from jax.experimental.pallas import tpu as pltpu

F32 = jnp.float32
BF16 = jnp.bfloat16

D_MODEL = 1024
CHUNK = 64
N_MEM = 256
MLA_HEADS = 8
Q_LORA = 384
KV_LORA = 256
QK_NOPE = 64
QK_ROPE = 32
V_HEAD = 64
MLA_WIDTH = MLA_HEADS * V_HEAD
MLA_SCALE = (QK_NOPE + QK_ROPE) ** -0.5
ROPE_THETA = 10000.0
LRU_WIDTH = 512
LRU_BLOCKS = 8
LRU_BLOCK = LRU_WIDTH // LRU_BLOCKS
CONV_W = 4
LRU_C = 8.0
MEM_HEADS = 4
MEM_HEAD_DIM = D_MODEL // MEM_HEADS
D_FF = 2816
EPS = 1e-6

LANES = 128
SUBLANES = 8
HEAD_PAD = LANES
MLA_PAD = MLA_HEADS * HEAD_PAD
VT_HEAD = V_HEAD + 16
VT_ROWS = MLA_HEADS * VT_HEAD
Q_SCALE = MLA_SCALE * 1.4426950408889634
QK_LOOKAHEAD = 2
ROW_STREAMS = 2
RESCALE_THRESHOLD = 64.0
FIRST_BLOCK_FLOOR = 64.0
ZC_Q = 0
ZC_KV = ZC_Q + Q_LORA
ZC_KPE = ZC_KV + KV_LORA
ZC_XBR = ZC_KPE + HEAD_PAD
ROPE_PARTNER_ROLL = HEAD_PAD - QK_ROPE
ZC_GBR = ZC_XBR + LRU_WIDTH
Z_COLS = ZC_GBR + LRU_WIDTH
NEG_BIG = -1e30
VMEM_LIMIT_BYTES = 56 * 1024 * 1024


def _const_spec(shape):
    nd = len(shape)
    return pl.BlockSpec(shape, lambda *_: (0,) * nd, pipeline_mode=pl.Buffered(1))


def _rms(x, g):
    ms = jnp.mean(x * x, axis=-1, keepdims=True)
    return x * lax.rsqrt(ms + EPS) * g


def _dot(a, b):
    return jnp.dot(a, b, preferred_element_type=F32)


def _dot_nt(a, b):
    return lax.dot_general(a, b, (((1,), (1,)), ((), ())), preferred_element_type=F32)


def _swiglu_half(xs, g_ref, w1_ref, w3_ref, w2_ref, ff_chunk):
    hs = [_rms(x, g_ref[...]).astype(BF16) for x in xs]
    accs = [jnp.zeros(x.shape, F32) for x in xs]
    for c in range(D_FF // ff_chunk):
        lo = c * ff_chunk
        ups = [(_dot(h, w1_ref[:, lo:lo + ff_chunk]), _dot(h, w3_ref[:, lo:lo + ff_chunk])) for h in hs]
        gs = [(a * jax.nn.sigmoid(a) * b).astype(BF16) for a, b in ups]
        accs = [acc + _dot(g, w2_ref[lo:lo + ff_chunk, :]) for acc, g in zip(accs, gs)]
    return [x + 0.5 * acc for x, acc in zip(xs, accs)]


def _row_streams(tm, n_streams):
    size = tm // n_streams
    return [(s * size, size) for s in range(n_streams)]


def _head_kernel(x_ref, cinit_ref, hinit_ref, g1_ref, w1_ref, w3_ref, w2_ref, gmix_ref, win_ref, gq_ref,
                 wuq_ref, gkv_ref, wukv_ref, wuvt_ref, freq_ref, sign_ref,
                 convw_ref, convb_ref, wg_ref, bg_ref, lam_ref, gl_ref,
                 x1_ref, q_ref, k_ref, v_ref, ckv_ref, kpe_ref, lru_ref, cstate_ref, hstate_ref,
                 xe_scr, ccar_scr, hcar_scr, zbuf_scr,
                 *, tm, seq_len, pos_base, ff_chunk, v_transposed, n_tiles, pipelined, n_streams):
    step = pl.program_id(0)
    lru_consts = (convw_ref, convb_ref, wg_ref, bg_ref, lam_ref, gl_ref)
    bcast = lambda hrow: jnp.broadcast_to(hrow, (SUBLANES, LRU_WIDTH))
    if pipelined:
        @pl.when(step == 0)
        def _():
            zbuf_scr[...] = jnp.zeros(zbuf_scr.shape, F32)
            ccar_scr[...] = jnp.zeros(ccar_scr.shape, F32)
            hcar_scr[...] = jnp.zeros(hcar_scr.shape, F32)

        new_seq = lax.rem(step - 1, seq_len // tm) == 0
        conv_prev = jnp.where(new_seq, cinit_ref[0], ccar_scr[...])
        h_prev = jnp.where(new_seq, hinit_ref[0, 0:1, :], hcar_scr[0:1, :])
        out, tail, h_last = _lru_block(zbuf_scr[:, :LRU_WIDTH], zbuf_scr[:, LRU_WIDTH:], conv_prev, h_prev,
                                       xe_scr, *lru_consts)
        lru_ref[...] = out
        ccar_scr[...] = tail
        hcar_scr[...] = bcast(h_last)
        cstate_ref[0] = tail
        hstate_ref[0] = bcast(h_last)
        tile = jnp.minimum(step, n_tiles - 1)
    else:
        tile = step

    streams = _row_streams(tm, n_streams)
    x1 = _swiglu_half([x_ref[lo:lo + sz, :] for lo, sz in streams], g1_ref, w1_ref, w3_ref, w2_ref, ff_chunk)
    x1 = x1[0] if n_streams == 1 else jnp.concatenate(x1, axis=0)
    x1_ref[...] = x1

    h = _rms(x1, gmix_ref[...]).astype(BF16)
    z = _dot(h, win_ref[...])

    row = lax.broadcasted_iota(jnp.int32, (tm, HEAD_PAD), 0) + tile * tm
    pos = (row & (seq_len - 1)) + pos_base
    ang = pos.astype(F32) * freq_ref[...]
    cos = jnp.cos(ang) * sign_ref[1:2, :]
    sin = jnp.sin(ang) * sign_ref[0:1, :]

    def rope(blk, c, s):
        return blk * c + pltpu.roll(blk, ROPE_PARTNER_ROLL, axis=1) * s

    cqn = _rms(z[:, ZC_Q:ZC_Q + Q_LORA], gq_ref[...]).astype(BF16)
    qz = _dot(cqn, wuq_ref[...])
    cos_q = cos * Q_SCALE
    sin_q = sin * Q_SCALE
    for hd in range(MLA_HEADS):
        lo = hd * HEAD_PAD
        q_ref[:, lo:lo + HEAD_PAD] = rope(qz[:, lo:lo + HEAD_PAD], cos_q, sin_q).astype(BF16)

    ckv = _rms(z[:, ZC_KV:ZC_KV + KV_LORA], gkv_ref[...])
    ckv_ref[...] = ckv
    kpe_blk = rope(z[:, ZC_KPE:ZC_KPE + HEAD_PAD], cos, sin)
    kpe_ref[...] = kpe_blk[:, QK_NOPE:QK_NOPE + QK_ROPE]

    ckv_b = ckv.astype(BF16)
    kz = _dot(ckv_b, wukv_ref[:, :MLA_PAD])
    for hd in range(MLA_HEADS):
        lo = hd * HEAD_PAD
        k_ref[:, lo:lo + HEAD_PAD] = (kz[:, lo:lo + HEAD_PAD] + kpe_blk).astype(BF16)
    if v_transposed:
        vt = _dot_nt(wuvt_ref[...], ckv_b)
        ridx = lax.broadcasted_iota(jnp.int32, vt.shape, 0)
        is_one = ridx == V_HEAD
        for hd in range(1, MLA_HEADS):
            is_one = is_one | (ridx == hd * VT_HEAD + V_HEAD)
        v_ref[...] = jnp.where(is_one, 1.0, vt).astype(BF16)
    else:
        v_ref[...] = _dot(ckv_b, wukv_ref[:, MLA_PAD:]).astype(BF16)

    xbr = z[:, ZC_XBR:ZC_XBR + LRU_WIDTH]
    gbr = z[:, ZC_GBR:ZC_GBR + LRU_WIDTH]
    if pipelined:
        zbuf_scr[:, :LRU_WIDTH] = xbr
        zbuf_scr[:, LRU_WIDTH:] = gbr
    elif seq_len >= tm:
        @pl.when(lax.rem(step, seq_len // tm) == 0)
        def _():
            ccar_scr[...] = cinit_ref[0]
            hcar_scr[...] = hinit_ref[0]

        out, tail, h_last = _lru_block(xbr, gbr, ccar_scr[...], hcar_scr[0:1, :], xe_scr, *lru_consts)
        ccar_scr[...] = tail
        hcar_scr[...] = bcast(h_last)
        lru_ref[...] = out
        cstate_ref[0] = tail
        hstate_ref[0] = bcast(h_last)
    else:
        for b in range(tm // seq_len):
            lo = b * seq_len
            out, tail, h_last = _lru_block(xbr[lo:lo + seq_len, :], gbr[lo:lo + seq_len, :], cinit_ref[b],
                                           hinit_ref[b, 0:1, :], xe_scr, *lru_consts)
            lru_ref[lo:lo + seq_len, :] = out
            cstate_ref[b] = tail
            hstate_ref[b] = bcast(h_last)


def _head_call(x, conv_init, h_init, wts, *, seq_len, pos_base, tm, ff_chunk, v_transposed):
    n = x.shape[0]
    nseq = n // seq_len
    assert n % tm == 0 and seq_len & (seq_len - 1) == 0 and seq_len >= SUBLANES
    assert seq_len % tm == 0 or tm % seq_len == 0
    nsub = max(1, tm // seq_len)
    n_tiles = n // tm
    pipelined = seq_len > tm
    if pipelined:
        row = lambda w: pl.BlockSpec((tm, w), lambda i: (jnp.minimum(i, n_tiles - 1), 0))
        lru_tile = lambda i: jnp.maximum(i - 1, 0)
        state = pl.BlockSpec((1, SUBLANES, LRU_WIDTH), lambda i: ((lru_tile(i) * tm) // seq_len, 0, 0))
    else:
        row = lambda w: pl.BlockSpec((tm, w), lambda i: (i, 0))
        state = pl.BlockSpec((nsub, SUBLANES, LRU_WIDTH), lambda i: ((i * tm) // (seq_len * nsub), 0, 0))
    consts = [wts['ffn1_norm'], wts['ffn1_w1'], wts['ffn1_w3'], wts['ffn1_w2'], wts['mix_norm'],
              wts['w_in_ext'], wts['q_norm'], wts['w_uq_ext'], wts['kv_norm'], wts['w_ukv_ext'],
              wts['w_uvt_ext'], wts['rope_freq'], wts['rope_sign'],
              wts['conv_w'], wts['conv_b'], wts['lru_wg'], wts['lru_bg'], wts['lru_lambda'],
              wts['lru_out_norm']]
    v_shape = (VT_ROWS, n) if v_transposed else (n, MLA_PAD)
    if v_transposed:
        v_spec = pl.BlockSpec((VT_ROWS, tm), lambda i: (0, jnp.minimum(i, n_tiles - 1)))
    else:
        v_spec = row(MLA_PAD)
    state_shape = jax.ShapeDtypeStruct((nseq, SUBLANES, LRU_WIDTH), F32)
    out_shapes = [
        jax.ShapeDtypeStruct((n, D_MODEL), F32),
        jax.ShapeDtypeStruct((n, MLA_PAD), BF16),
        jax.ShapeDtypeStruct((n, MLA_PAD), BF16),
        jax.ShapeDtypeStruct(v_shape, BF16),
        jax.ShapeDtypeStruct((n, KV_LORA), F32),
        jax.ShapeDtypeStruct((n, QK_ROPE), F32),
        jax.ShapeDtypeStruct((n, LRU_WIDTH), BF16),
        state_shape,
        state_shape,
    ]
    out_specs = [row(s.shape[1]) for s in out_shapes[:7]] + [state, state]
    out_specs[3] = v_spec
    if pipelined:
        out_specs[6] = pl.BlockSpec((tm, LRU_WIDTH), lambda i: (lru_tile(i), 0))
    lru_rows = min(tm, seq_len)
    return pl.pallas_call(
        functools.partial(_head_kernel, tm=tm, seq_len=seq_len, pos_base=pos_base, ff_chunk=ff_chunk,
                          v_transposed=v_transposed, n_tiles=n_tiles, pipelined=pipelined,
                          n_streams=ROW_STREAMS if tm % (ROW_STREAMS * 16) == 0 and tm >= 512 else 1),
        grid=(n_tiles + 1 if pipelined else n_tiles,),
        in_specs=[row(D_MODEL), state, state] + [_const_spec(c.shape) for c in consts],
        out_specs=out_specs,
        out_shape=out_shapes,
        scratch_shapes=[pltpu.VMEM((lru_rows + SUBLANES, LRU_WIDTH), F32),
                        pltpu.VMEM((SUBLANES, LRU_WIDTH), F32),
                        pltpu.VMEM((SUBLANES, LRU_WIDTH), F32),
                        pltpu.VMEM((tm if pipelined else SUBLANES, 2 * LRU_WIDTH), F32)],
        compiler_params=pltpu.CompilerParams(dimension_semantics=("arbitrary",),
                                             vmem_limit_bytes=VMEM_LIMIT_BYTES),
        name="head",
    )(x, conv_init, h_init, *consts)


def _attn_finalize(o_heads, gattn_ref, out_ref):
    ss = o_heads[0] * o_heads[0]
    for o in o_heads[1:]:
        ss = ss + o * o
    ms = jnp.sum(ss, axis=-1, keepdims=True) * (1.0 / MLA_WIDTH)
    inv = lax.rsqrt(ms + EPS)
    for hd, o in enumerate(o_heads):
        lo = hd * HEAD_PAD
        out_ref[:, lo:lo + HEAD_PAD] = (o * inv * gattn_ref[:, lo:lo + HEAD_PAD]).astype(BF16)


def _mla_prompt_step(q_ref, k_ref, vt_ref, m_scr, mx_scr, acc_scr, cur, first, visible, *, rescale,
                     keys=None, queries=None, second_part=False):
    k0, k1 = keys if keys is not None else (0, k_ref.shape[0])
    q0, q1 = queries if queries is not None else (0, q_ref.shape[0])

    def scores(hd):
        lo = hd * HEAD_PAD
        return _dot_nt(k_ref[k0:k1, lo:lo + HEAD_PAD], q_ref[q0:q1, lo:lo + HEAD_PAD])

    pending = [scores(hd) for hd in range(QK_LOOKAHEAD)]
    for hd in range(MLA_HEADS):
        st = pending[hd]
        if hd + QK_LOOKAHEAD < MLA_HEADS:
            pending.append(scores(hd + QK_LOOKAHEAD))
        if visible is not None:
            st = jnp.where(visible, st, NEG_BIG)
        vo = hd * VT_HEAD
        blk_max = jnp.max(st, axis=0, keepdims=True)
        if rescale:
            m_prev = jnp.where(first, NEG_BIG, m_scr[hd:hd + 1, q0:q1])
            m_new = jnp.maximum(m_prev, blk_max)
            alpha = jnp.exp2(m_prev - m_new)
            pt = jnp.exp2(st - m_new).astype(BF16)
            acc_scr[cur, vo:vo + VT_HEAD, q0:q1] = (alpha * acc_scr[cur, vo:vo + VT_HEAD, q0:q1]
                                                    + _dot(vt_ref[vo:vo + VT_HEAD, k0:k1], pt))
            m_scr[hd:hd + 1, q0:q1] = m_new
        else:
            src = 1 - cur if second_part else cur
            mx_scr[hd:hd + 1, q0:q1] = jnp.maximum(mx_scr[hd:hd + 1, q0:q1], blk_max) if second_part else blk_max
            pt = jnp.exp2(st - m_scr[hd:hd + 1, q0:q1]).astype(BF16)
            acc_scr[1 - cur, vo:vo + VT_HEAD, q0:q1] = (acc_scr[src, vo:vo + VT_HEAD, q0:q1]
                                                        + _dot(vt_ref[vo:vo + VT_HEAD, k0:k1], pt))


def _mla_prompt_kernel(qi_ref, ki_ref, q_ref, k_ref, vt_ref, gattn_ref, out_ref,
                       m_scr, mx_scr, acc_scr, cur_scr, *, tq, tk):
    step = pl.program_id(0)
    q_lo = qi_ref[step] * tq
    k_lo = ki_ref[step] * tk
    first = k_lo == 0

    @pl.when(first)
    def _():
        m_scr[...] = jnp.zeros(m_scr.shape, F32)
        acc_scr[0] = jnp.zeros(acc_scr.shape[1:], F32)
        cur_scr[0] = 0

    cur = cur_scr[0]
    fully_visible = k_lo + tk <= q_lo + CHUNK

    def chunk_mask():
        kc = (lax.broadcasted_iota(jnp.int32, (tk, tq), 0) + k_lo) // CHUNK
        qc = (lax.broadcasted_iota(jnp.int32, (tk, tq), 1) + q_lo) // CHUNK
        return kc <= qc

    step_fn = functools.partial(_mla_prompt_step, q_ref, k_ref, vt_ref, m_scr, mx_scr, acc_scr, cur, first)

    @pl.when(fully_visible)
    def _():
        step_fn(None, rescale=False)

    @pl.when(jnp.logical_not(fully_visible))
    def _():
        if tq == tk:
            vis = chunk_mask()
            step_fn(vis[:tk // 2, :], rescale=False, keys=(0, tk // 2))
            step_fn(vis[tk // 2:, tq // 2:], rescale=False, keys=(tk // 2, tk), queries=(tq // 2, tq),
                    second_part=True)
        else:
            step_fn(chunk_mask(), rescale=False)

    excess = mx_scr[...] - m_scr[...]
    redo = (jnp.max(excess) > RESCALE_THRESHOLD) | (first & (jnp.min(excess) < -FIRST_BLOCK_FLOOR))

    @pl.when(redo)
    def _():
        step_fn(chunk_mask(), rescale=True)

    @pl.when(jnp.logical_not(redo))
    def _():
        cur_scr[0] = 1 - cur

    @pl.when(k_lo + tk >= q_lo + tq)
    def _():
        fin = cur_scr[0]
        parts = []
        for hd in range(MLA_HEADS):
            vo = hd * VT_HEAD
            parts.append(acc_scr[fin, vo:vo + V_HEAD, :] / acc_scr[fin, vo + V_HEAD:vo + V_HEAD + 1, :])
        o = jnp.concatenate(parts, axis=0).T
        out_ref[...] = _rms(o, gattn_ref[...]).astype(BF16)


def _mla_prompt_call(q, k, vt, gattn, *, tq, tk):
    n = q.shape[0]
    assert n % tq == 0 and tq % tk == 0 and tk % CHUNK == 0
    pairs = [(qi, ki) for qi in range(n // tq) for ki in range((qi + 1) * tq // tk)]
    qi_tab = jnp.asarray([p[0] for p in pairs], jnp.int32)
    ki_tab = jnp.asarray([p[1] for p in pairs], jnp.int32)
    grid_spec = pltpu.PrefetchScalarGridSpec(
        num_scalar_prefetch=2,
        grid=(len(pairs),),
        in_specs=[pl.BlockSpec((tq, MLA_PAD), lambda s, qi, ki: (qi[s], 0)),
                  pl.BlockSpec((tk, MLA_PAD), lambda s, qi, ki: (ki[s], 0)),
                  pl.BlockSpec((VT_ROWS, tk), lambda s, qi, ki: (0, ki[s])),
                  pl.BlockSpec(gattn.shape, lambda s, qi, ki: (0, 0))],
        out_specs=pl.BlockSpec((tq, MLA_WIDTH), lambda s, qi, ki: (qi[s], 0)),
        scratch_shapes=[pltpu.VMEM((MLA_HEADS, tq), F32),
                        pltpu.VMEM((MLA_HEADS, tq), F32),
                        pltpu.VMEM((2, VT_ROWS, tq), F32),
                        pltpu.SMEM((1,), jnp.int32)],
    )
    return pl.pallas_call(
        functools.partial(_mla_prompt_kernel, tq=tq, tk=tk),
        grid_spec=grid_spec,
        out_shape=jax.ShapeDtypeStruct((n, MLA_WIDTH), BF16),
        compiler_params=pltpu.CompilerParams(dimension_semantics=("arbitrary",),
                                             vmem_limit_bytes=VMEM_LIMIT_BYTES),
        name="mla_prompt",
    )(qi_tab, ki_tab, q, k, vt, gattn)


def _mla_sample_kernel(q_ref, kn_ref, vn_ref, ckv_ref, kpe_ref, wukv_ref, gattn_ref, out_ref):
    t = q_ref.shape[0]
    heads = [slice(hd * HEAD_PAD, (hd + 1) * HEAD_PAD) for hd in range(MLA_HEADS)]
    cp = ckv_ref[0].astype(BF16)
    past = cp.shape[0]
    kpe = jnp.concatenate([jnp.zeros((past, QK_NOPE), F32), kpe_ref[0],
                           jnp.zeros((past, HEAD_PAD - QK_NOPE - QK_ROPE), F32)], axis=1).astype(BF16)
    q_all = jnp.concatenate([q_ref[:, hs] for hs in heads], axis=0)
    q_lat = jnp.concatenate([_dot_nt(q_ref[:, hs], wukv_ref[:, hs]) for hs in heads], axis=0)
    s1 = _dot_nt(q_lat.astype(BF16), cp) + _dot_nt(q_all, kpe)
    s2 = jnp.concatenate([_dot_nt(q_ref[:, hs], kn_ref[:, hs]) for hs in heads], axis=0)
    m = jnp.maximum(jnp.max(s1, axis=-1, keepdims=True), jnp.max(s2, axis=-1, keepdims=True))
    p1 = jnp.exp2(s1 - m)
    p2 = jnp.exp2(s2 - m)
    l = jnp.sum(p1, axis=-1, keepdims=True) + jnp.sum(p2, axis=-1, keepdims=True)
    p2 = p2.astype(BF16)
    o_lat = _dot(p1.astype(BF16), cp).astype(BF16)
    o_heads = []
    for hd, hs in enumerate(heads):
        rows = slice(hd * t, (hd + 1) * t)
        w_uv = wukv_ref[:, MLA_PAD + hd * HEAD_PAD:MLA_PAD + (hd + 1) * HEAD_PAD]
        o = _dot(o_lat[rows, :], w_uv) + _dot(p2[rows, :], vn_ref[:, hs])
        o_heads.append(o / l[rows, :])
    _attn_finalize(o_heads, gattn_ref, out_ref)


def _mla_sample_call(q, k, v, ckv_past, kpe_past, w_ukv_ext, gattn, *, t_new):
    n = q.shape[0]
    nbatch, past, _ = ckv_past.shape
    assert n == nbatch * t_new
    rows = pl.BlockSpec((t_new, MLA_PAD), lambda b: (b, 0))
    return pl.pallas_call(
        _mla_sample_kernel,
        grid=(nbatch,),
        in_specs=[rows, rows, rows,
                  pl.BlockSpec((1, past, KV_LORA), lambda b: (b, 0, 0)),
                  pl.BlockSpec((1, past, QK_ROPE), lambda b: (b, 0, 0)),
                  _const_spec(w_ukv_ext.shape), _const_spec(gattn.shape)],
        out_specs=rows,
        out_shape=jax.ShapeDtypeStruct((n, MLA_PAD), BF16),
        compiler_params=pltpu.CompilerParams(dimension_semantics=("arbitrary",),
                                             vmem_limit_bytes=VMEM_LIMIT_BYTES),
        name="mla_sample",
    )(q, k, v, ckv_past, kpe_past, w_ukv_ext, gattn)


def _lru_block(x, gate, conv_prev, h_prev, xe_scr, convw_ref, convb_ref, wg_ref, bg_ref, lam_ref, gl_ref):
    tm = x.shape[0]
    xe_scr[0:SUBLANES, :] = conv_prev
    xe_scr[SUBLANES:SUBLANES + tm, :] = x
    xc = convb_ref[...] + xe_scr[pl.ds(SUBLANES - 3, tm), :] * convw_ref[0:1, :]
    xc = xc + xe_scr[pl.ds(SUBLANES - 2, tm), :] * convw_ref[1:2, :]
    xc = xc + xe_scr[pl.ds(SUBLANES - 1, tm), :] * convw_ref[2:3, :]
    xc = xc + x * convw_ref[3:4, :]
    tail = x[tm - SUBLANES:, :]

    y = _dot(xc.astype(BF16), wg_ref[...]) + bg_ref[...]
    r = jax.nn.sigmoid(y[:, :LRU_WIDTH])
    ig = jax.nn.sigmoid(y[:, LRU_WIDTH:])
    nl = -lam_ref[...]
    softplus = jnp.maximum(nl, 0.0) + jnp.log1p(jnp.exp(-jnp.abs(nl)))
    log_a = -LRU_C * r * softplus
    a = jnp.exp(log_a)
    th = jnp.tanh(log_a)
    one_minus_a2 = -2.0 * th / (1.0 - th)
    b = jnp.sqrt(one_minus_a2) * ig * xc

    sub = lax.broadcasted_iota(jnp.int32, (tm, LRU_WIDTH), 0) & (SUBLANES - 1)
    s = 1
    while s < SUBLANES:
        valid = sub >= s
        a_sh = pltpu.roll(a, s, axis=0)
        b_sh = pltpu.roll(b, s, axis=0)
        b = jnp.where(valid, a * b_sh + b, b)
        a = jnp.where(valid, a * a_sh, a)
        s *= 2
    h_groups = []
    for g in range(tm // SUBLANES):
        lo = g * SUBLANES
        hg = b[lo:lo + SUBLANES, :] + a[lo:lo + SUBLANES, :] * h_prev
        h_groups.append(hg)
        h_prev = hg[SUBLANES - 1:SUBLANES, :]
    h = jnp.concatenate(h_groups, axis=0)
    out = _rms(jax.nn.gelu(gate) * h, gl_ref[...]).astype(BF16)
    return out, tail, h_prev


def _tail_kernel(x1_ref, attn_ref, lru_ref, woa_ref, wol_ref, gx_ref, wmq_ref, mk_ref, mv_ref, wmo_ref,
                 g2_ref, w1_ref, w3_ref, w2_ref, gf_ref, y_ref, *, tm, nb, ff_chunk, n_streams):
    streams = _row_streams(tm, n_streams)
    x2s = [x1_ref[lo:lo + sz, :] + _dot(attn_ref[lo:lo + sz, :], woa_ref[...])
           + _dot(lru_ref[lo:lo + sz, :], wol_ref[...]) for lo, sz in streams]

    hqs = [_rms(x2, gx_ref[...]).astype(BF16) for x2 in x2s]
    qs = [(_dot(hq, wmq_ref[...]) * (MEM_HEAD_DIM ** -0.5)).astype(BF16) for hq in hqs]

    def cross_attend(q, mem_batch):
        o_heads = []
        for hd in range(MEM_HEADS):
            lo = hd * MEM_HEAD_DIM
            s = _dot_nt(q[:, lo:lo + MEM_HEAD_DIM], mk_ref[mem_batch, :, lo:lo + MEM_HEAD_DIM])
            p = jnp.exp(s - jnp.max(s, axis=-1, keepdims=True))
            l = jnp.sum(p, axis=-1, keepdims=True)
            o_heads.append(_dot(p.astype(BF16), mv_ref[mem_batch, :, lo:lo + MEM_HEAD_DIM]) / l)
        return jnp.concatenate(o_heads, axis=1)

    if nb == 1:
        os_ = [cross_attend(q, 0) for q in qs]
    else:
        rows = tm // nb
        os_ = [jnp.concatenate([cross_attend(qs[0][b * rows:(b + 1) * rows, :], b) for b in range(nb)], axis=0)]
    x3s = [x2 + _dot(o.astype(BF16), wmo_ref[...]) for x2, o in zip(x2s, os_)]

    x4s = _swiglu_half(x3s, g2_ref, w1_ref, w3_ref, w2_ref, ff_chunk)
    for (lo, sz), x4 in zip(streams, x4s):
        y_ref[lo:lo + sz, :] = _rms(x4, gf_ref[...])


def _tail_call(x1, attn, lru, mk, mv, wts, *, seq_len, tm, nb, ff_chunk):
    n = x1.shape[0]
    assert n % tm == 0 and tm % nb == 0 and mk.shape[0] * seq_len == n
    assert (nb == 1 and seq_len % tm == 0) or nb * seq_len == tm
    row = lambda w: pl.BlockSpec((tm, w), lambda i: (i, 0))
    mem = pl.BlockSpec((nb, N_MEM, D_MODEL), lambda i: ((i * tm) // (seq_len * nb), 0, 0))
    c = lambda name: _const_spec(wts[name].shape)
    w_out_attn = wts['w_out_attn'] if attn.shape[1] == MLA_WIDTH else wts['w_out_attn_pad']
    return pl.pallas_call(
        functools.partial(_tail_kernel, tm=tm, nb=nb, ff_chunk=ff_chunk,
                          n_streams=ROW_STREAMS if nb == 1 and tm % (ROW_STREAMS * 16) == 0 else 1),
        grid=(n // tm,),
        in_specs=[row(D_MODEL), row(attn.shape[1]), row(LRU_WIDTH), _const_spec(w_out_attn.shape),
                  c('w_out_lru'), c('xattn_norm'), c('w_mq'),
                  mem, mem, c('w_mo'), c('ffn2_norm'), c('ffn2_w1'), c('ffn2_w3'), c('ffn2_w2'),
                  c('final_norm')],
        out_specs=row(D_MODEL),
        out_shape=jax.ShapeDtypeStruct((n, D_MODEL), F32),
        compiler_params=pltpu.CompilerParams(dimension_semantics=("arbitrary",),
                                             vmem_limit_bytes=VMEM_LIMIT_BYTES),
        name="tail",
    )(x1, attn, lru, w_out_attn, wts['w_out_lru'], wts['xattn_norm'], wts['w_mq'], mk, mv, wts['w_mo'],
      wts['ffn2_norm'], wts['ffn2_w1'], wts['ffn2_w3'], wts['ffn2_w2'], wts['final_norm'])


def _memkv_kernel(mem_ref, g_ref, wmk_ref, wmv_ref, k_ref, v_ref, kb_ref, vb_ref):
    m = _rms(mem_ref[...], g_ref[...]).astype(BF16)
    k = _dot(m, wmk_ref[...])
    v = _dot(m, wmv_ref[...])
    k_ref[...] = k
    v_ref[...] = v
    kb_ref[...] = k.astype(BF16)
    vb_ref[...] = v.astype(BF16)


def _memkv_call(mem, wts):
    n = mem.shape[0]
    full = lambda dt: jax.ShapeDtypeStruct((n, D_MODEL), dt)
    spec = pl.BlockSpec((n, D_MODEL), lambda i: (0, 0))
    return pl.pallas_call(
        _memkv_kernel,
        grid=(1,),
        in_specs=[spec, _const_spec(wts['mem_norm'].shape), _const_spec(wts['w_mk'].shape),
                  _const_spec(wts['w_mv'].shape)],
        out_specs=[spec, spec, spec, spec],
        out_shape=[full(F32), full(F32), full(BF16), full(BF16)],
        compiler_params=pltpu.CompilerParams(dimension_semantics=("arbitrary",),
                                             vmem_limit_bytes=VMEM_LIMIT_BYTES),
        name="memkv",
    )(mem, wts['mem_norm'], wts['w_mk'], wts['w_mv'])


CAST_STEPS = 8


def _cast_kernel(*refs):
    n = len(refs) // 2
    for w_ref, o_ref in zip(refs[:n], refs[n:]):
        o_ref[...] = w_ref[...].astype(BF16)


def _cast_bf16_call(ws):
    specs = []
    for w in ws:
        rows, cols = w.shape
        assert rows % (CAST_STEPS * 16) == 0 and cols % LANES == 0
        specs.append(pl.BlockSpec((rows // CAST_STEPS, cols), lambda i: (i, 0)))
    return pl.pallas_call(
        _cast_kernel,
        grid=(CAST_STEPS,),
        in_specs=specs,
        out_specs=specs,
        out_shape=[jax.ShapeDtypeStruct(w.shape, BF16) for w in ws],
        compiler_params=pltpu.CompilerParams(dimension_semantics=("arbitrary",),
                                             vmem_limit_bytes=VMEM_LIMIT_BYTES),
        name="cast_bf16",
    )(*ws)


def _prep_weights(p):
    w = {}
    rowv = lambda a: a.reshape(1, -1).astype(F32)
    for name in ('ffn1_norm', 'mix_norm', 'q_norm', 'kv_norm', 'xattn_norm', 'ffn2_norm', 'final_norm',
                 'mem_norm', 'lru_out_norm', 'conv_b', 'lru_lambda'):
        w[name] = rowv(p[name])
    big = ('ffn1_w1', 'ffn1_w3', 'ffn1_w2', 'ffn2_w1', 'ffn2_w3', 'ffn2_w2', 'w_mq', 'w_mk', 'w_mv', 'w_mo')
    w.update(zip(big, _cast_bf16_call([p[name] for name in big])))

    half = QK_ROPE // 2
    w_in = p['w_in']
    kpe_cols = w_in[:, ZC_KPE:ZC_KPE + QK_ROPE]
    assert HEAD_PAD == QK_NOPE + 2 * QK_ROPE
    pad_l = jnp.zeros((D_MODEL, QK_NOPE), F32)
    kpe_blk = jnp.concatenate([pad_l, kpe_cols, kpe_cols[:, half:], kpe_cols[:, :half]], axis=1)
    lru_cols = w_in[:, Q_LORA + KV_LORA + QK_ROPE:]
    w['w_in_ext'] = jnp.concatenate([w_in[:, :Q_LORA + KV_LORA], kpe_blk, lru_cols], axis=1).astype(BF16)

    wq = p['w_uq'].reshape(Q_LORA, MLA_HEADS, QK_NOPE + QK_ROPE)
    q_blk = jnp.concatenate([wq, wq[:, :, QK_NOPE + half:], wq[:, :, QK_NOPE:QK_NOPE + half]], axis=2)
    w['w_uq_ext'] = q_blk.reshape(Q_LORA, MLA_PAD).astype(BF16)

    wkv = p['w_ukv'].reshape(KV_LORA, MLA_HEADS, QK_NOPE + V_HEAD)
    zk = jnp.zeros((KV_LORA, MLA_HEADS, HEAD_PAD - QK_NOPE), F32)
    zv = jnp.zeros((KV_LORA, MLA_HEADS, HEAD_PAD - V_HEAD), F32)
    k_part = jnp.concatenate([wkv[:, :, :QK_NOPE], zk], axis=2).reshape(KV_LORA, MLA_PAD)
    v_part = jnp.concatenate([wkv[:, :, QK_NOPE:], zv], axis=2).reshape(KV_LORA, MLA_PAD)
    w['w_ukv_ext'] = jnp.concatenate([k_part, v_part], axis=1).astype(BF16)
    vt = jnp.concatenate([wkv[:, :, QK_NOPE:], jnp.zeros((KV_LORA, MLA_HEADS, VT_HEAD - V_HEAD), F32)], axis=2)
    w['w_uvt_ext'] = vt.reshape(KV_LORA, VT_ROWS).T.astype(BF16)

    inv_freq = ROPE_THETA ** (-jnp.arange(0, QK_ROPE, 2, dtype=F32) / QK_ROPE)
    zl = jnp.zeros((QK_NOPE,), F32)
    zr = jnp.zeros((HEAD_PAD - QK_NOPE - QK_ROPE,), F32)
    w['rope_freq'] = jnp.concatenate([zl, inv_freq, inv_freq, zr]).reshape(1, HEAD_PAD)
    sign = jnp.concatenate([zl, -jnp.ones((half,), F32), jnp.ones((half,), F32), zr])
    keep = jnp.concatenate([jnp.ones((QK_NOPE + QK_ROPE,), F32), zr])
    w['rope_sign'] = jnp.stack([sign, keep])

    g_attn = jnp.concatenate([p['attn_out_norm'].reshape(MLA_HEADS, V_HEAD),
                              jnp.zeros((MLA_HEADS, HEAD_PAD - V_HEAD), F32)], axis=1)
    w['attn_out_norm_pad'] = g_attn.reshape(1, MLA_PAD)
    w['attn_out_norm'] = rowv(p['attn_out_norm'])

    w_out = p['w_out']
    wo_attn = jnp.concatenate([w_out[:MLA_WIDTH].reshape(MLA_HEADS, V_HEAD, D_MODEL),
                               jnp.zeros((MLA_HEADS, HEAD_PAD - V_HEAD, D_MODEL), F32)], axis=1)
    w['w_out_attn_pad'] = wo_attn.reshape(MLA_PAD, D_MODEL).astype(BF16)
    w['w_out_attn'] = w_out[:MLA_WIDTH].astype(BF16)
    w['w_out_lru'] = w_out[MLA_WIDTH:].astype(BF16)

    def block_diag(wb):
        eye = jnp.eye(LRU_BLOCKS, dtype=F32)
        return (eye[:, None, :, None] * wb[:, :, None, :]).reshape(LRU_WIDTH, LRU_WIDTH)

    w['lru_wg'] = jnp.concatenate([block_diag(p['lru_wa']), block_diag(p['lru_wx'])], axis=1).astype(BF16)
    w['lru_bg'] = jnp.concatenate([p['lru_ba'].reshape(1, -1), p['lru_bx'].reshape(1, -1)], axis=1).astype(F32)
    w['conv_w'] = p['conv_w'].astype(F32)
    return w


def _pick_tile(n, cap):
    t = min(n, cap)
    while n % t:
        t //= 2
    return t


def _group(x, mk, mv, wts, *, past, conv_init, h_init):
    nbatch, seq, _ = x.shape
    n = nbatch * seq
    xf = x.reshape(n, D_MODEL)
    tm = _pick_tile(n, 512)
    ff_chunk = D_FF // 2
    pos_base = 0 if past is None else past[0].shape[1]
    x1, q, k, v, ckv, kpe, lru, cstate, hstate = _head_call(
        xf, conv_init, h_init, wts, seq_len=seq, pos_base=pos_base, tm=tm, ff_chunk=ff_chunk,
        v_transposed=past is None)
    if past is None:
        assert nbatch == 1
        tq = _pick_tile(n, 1024)
        attn = _mla_prompt_call(q, k, v, wts['attn_out_norm'], tq=tq, tk=tq)
    else:
        ckv_past, kpe_past = past
        attn = _mla_sample_call(q, k, v, ckv_past, kpe_past, wts['w_ukv_ext'], wts['attn_out_norm_pad'],
                                t_new=seq)
    nb = tm // seq if seq < tm else 1
    y = _tail_call(x1, attn, lru, mk, mv, wts, seq_len=seq, tm=tm, nb=nb, ff_chunk=ff_chunk)
    return (y.reshape(nbatch, seq, D_MODEL), ckv.reshape(nbatch, seq, KV_LORA),
            kpe.reshape(nbatch, seq, QK_ROPE), cstate[:, SUBLANES - (CONV_W - 1):, :], hstate[:, 0, :])


def kernel(x_prompt, x_sample, mem_prompt, cache_mla_ckv, cache_mla_kpe, state_conv, state_lru, cache_mem_k, cache_mem_v, ffn1_norm, ffn1_w1, ffn1_w3, ffn1_w2, mix_norm, w_in, q_norm, w_uq, kv_norm, w_ukv, conv_w, conv_b, lru_wa, lru_ba, lru_wx, lru_bx, lru_lambda, attn_out_norm, lru_out_norm, w_out, mem_norm, xattn_norm, w_mq, w_mk, w_mv, w_mo, ffn2_norm, ffn2_w1, ffn2_w3, ffn2_w2, final_norm):
    depth = ffn1_norm.shape[0]
    assert depth == 1
    params = dict(ffn1_norm=ffn1_norm[0], ffn1_w1=ffn1_w1[0], ffn1_w3=ffn1_w3[0], ffn1_w2=ffn1_w2[0],
                  mix_norm=mix_norm[0], w_in=w_in[0], q_norm=q_norm[0], w_uq=w_uq[0], kv_norm=kv_norm[0],
                  w_ukv=w_ukv[0], conv_w=conv_w[0], conv_b=conv_b[0], lru_wa=lru_wa[0], lru_ba=lru_ba[0],
                  lru_wx=lru_wx[0], lru_bx=lru_bx[0], lru_lambda=lru_lambda[0],
                  attn_out_norm=attn_out_norm[0], lru_out_norm=lru_out_norm[0], w_out=w_out[0],
                  mem_norm=mem_norm[0], xattn_norm=xattn_norm[0], w_mq=w_mq[0], w_mk=w_mk[0], w_mv=w_mv[0],
                  w_mo=w_mo[0], ffn2_norm=ffn2_norm[0], ffn2_w1=ffn2_w1[0], ffn2_w3=ffn2_w3[0],
                  ffn2_w2=ffn2_w2[0], final_norm=final_norm)
    wts = _prep_weights(params)

    bp = x_prompt.shape[0]
    bs = x_sample.shape[0]
    n_mem = mem_prompt.shape[1]

    assert bp == 1
    mk_f, mv_f, mk_b, mv_b = _memkv_call(mem_prompt.reshape(bp * n_mem, D_MODEL), wts)
    zero_state = jnp.zeros((bp, SUBLANES, LRU_WIDTH), F32)
    y_p, ckv_p, kpe_p, conv_p, lru_p = _group(
        x_prompt, mk_b.reshape(bp, n_mem, D_MODEL), mv_b.reshape(bp, n_mem, D_MODEL), wts,
        past=None, conv_init=zero_state, h_init=zero_state)

    conv_init = jnp.pad(state_conv[0], ((0, 0), (SUBLANES - (CONV_W - 1), 0), (0, 0)))
    h_init = jnp.broadcast_to(state_lru[0][:, None, :], (bs, SUBLANES, LRU_WIDTH))
    y_s, ckv_s, kpe_s, conv_s, lru_s = _group(
        x_sample, cache_mem_k[0].reshape(bs, n_mem, D_MODEL).astype(BF16),
        cache_mem_v[0].reshape(bs, n_mem, D_MODEL).astype(BF16), wts,
        past=(cache_mla_ckv[0], cache_mla_kpe[0]), conv_init=conv_init, h_init=h_init)

    mem_shape = (1, bp, n_mem, MEM_HEADS, MEM_HEAD_DIM)
    return (y_p, y_s, ckv_p[None], kpe_p[None], conv_p[None], lru_p[None],
            mk_f.reshape(mem_shape), mv_f.reshape(mem_shape),
            ckv_s[None], kpe_s[None], conv_s[None], lru_s[None])
```

```python
import functools

import jax
import jax.numpy as jnp
from jax import lax
from jax.experimental import pallas as pl
from jax.experimental.pallas import tpu as pltpu

F32 = jnp.float32
BF16 = jnp.bfloat16

D_MODEL = 1024
CHUNK = 64
N_MEM = 256
MLA_HEADS = 8
Q_LORA = 384
KV_LORA = 256
QK_NOPE = 64
QK_ROPE = 32
V_HEAD = 64
MLA_WIDTH = MLA_HEADS * V_HEAD
MLA_SCALE = (QK_NOPE + QK_ROPE) ** -0.5
ROPE_THETA = 10000.0
LRU_WIDTH = 512
LRU_BLOCKS = 8
LRU_BLOCK = LRU_WIDTH // LRU_BLOCKS
CONV_W = 4
LRU_C = 8.0
MEM_HEADS = 4
MEM_HEAD_DIM = D_MODEL // MEM_HEADS
D_FF = 2816
EPS = 1e-6

LANES = 128
SUBLANES = 8
HEAD_PAD = LANES
MLA_PAD = MLA_HEADS * HEAD_PAD
VT_HEAD = V_HEAD + 16
VT_ROWS = MLA_HEADS * VT_HEAD
Q_SCALE = MLA_SCALE * 1.4426950408889634
QK_LOOKAHEAD = 2
ROW_STREAMS = 2
RESCALE_THRESHOLD = 64.0
FIRST_BLOCK_FLOOR = 64.0
ZC_Q = 0
ZC_KV = ZC_Q + Q_LORA
ZC_KPE = ZC_KV + KV_LORA
ZC_XBR = ZC_KPE + HEAD_PAD
ZC_GBR = ZC_XBR + LRU_WIDTH
Z_COLS = ZC_GBR + LRU_WIDTH
ROPE_PARTNER_ROLL = HEAD_PAD - QK_ROPE
NEG_BIG = -1e30
VMEM_LIMIT_BYTES = 56 * 1024 * 1024


def _const_spec(shape):
    nd = len(shape)
    return pl.BlockSpec(shape, lambda *_: (0,) * nd, pipeline_mode=pl.Buffered(1))


def _rms(x, g):
    ms = jnp.mean(x * x, axis=-1, keepdims=True)
    return x * lax.rsqrt(ms + EPS) * g


def _dot(a, b):
    return jnp.dot(a, b, preferred_element_type=F32)


def _dot_nt(a, b):
    return lax.dot_general(a, b, (((1,), (1,)), ((), ())), preferred_element_type=F32)


def _swiglu_half(xs, g_ref, w1_ref, w3_ref, w2_ref, ff_chunk):
    hs = [_rms(x, g_ref[...]).astype(BF16) for x in xs]
    accs = [jnp.zeros(x.shape, F32) for x in xs]
    for c in range(D_FF // ff_chunk):
        lo = c * ff_chunk
        ups = [(_dot(h, w1_ref[:, lo:lo + ff_chunk]), _dot(h, w3_ref[:, lo:lo + ff_chunk])) for h in hs]
        gs = [(a * jax.nn.sigmoid(a) * b).astype(BF16) for a, b in ups]
        accs = [acc + _dot(g, w2_ref[lo:lo + ff_chunk, :]) for acc, g in zip(accs, gs)]
    return [x + 0.5 * acc for x, acc in zip(xs, accs)]


def _row_streams(tm, n_streams):
    size = tm // n_streams
    return [(s * size, size) for s in range(n_streams)]


def _head_kernel(x_ref, cinit_ref, hinit_ref, g1_ref, w1_ref, w3_ref, w2_ref, gmix_ref, win_ref, gq_ref,
                 wuq_ref, gkv_ref, wukv_ref, wuvt_ref, freq_ref, sign_ref,
                 convw_ref, convb_ref, wg_ref, bg_ref, lam_ref, gl_ref,
                 x1_ref, q_ref, k_ref, v_ref, ckv_ref, kpe_ref, lru_ref, cstate_ref, hstate_ref,
                 xe_scr, ccar_scr, hcar_scr, zbuf_scr,
                 *, tm, seq_len, pos_base, ff_chunk, v_transposed, n_tiles, pipelined, n_streams):
    step = pl.program_id(0)
    lru_consts = (convw_ref, convb_ref, wg_ref, bg_ref, lam_ref, gl_ref)
    bcast = lambda hrow: jnp.broadcast_to(hrow, (SUBLANES, LRU_WIDTH))
    if pipelined:
        @pl.when(step == 0)
        def _():
            zbuf_scr[...] = jnp.zeros(zbuf_scr.shape, F32)
            ccar_scr[...] = jnp.zeros(ccar_scr.shape, F32)
            hcar_scr[...] = jnp.zeros(hcar_scr.shape, F32)

        new_seq = lax.rem(step - 1, seq_len // tm) == 0
        conv_prev = jnp.where(new_seq, cinit_ref[0], ccar_scr[...])
        h_prev = jnp.where(new_seq, hinit_ref[0, 0:1, :], hcar_scr[0:1, :])
        out, tail, h_last = _lru_block(zbuf_scr[:, :LRU_WIDTH], zbuf_scr[:, LRU_WIDTH:], conv_prev, h_prev,
                                       xe_scr, *lru_consts)
        lru_ref[...] = out
        ccar_scr[...] = tail
        hcar_scr[...] = bcast(h_last)
        cstate_ref[0] = tail
        hstate_ref[0] = bcast(h_last)
        tile = jnp.minimum(step, n_tiles - 1)
    else:
        tile = step

    streams = _row_streams(tm, n_streams)
    x1 = _swiglu_half([x_ref[lo:lo + sz, :] for lo, sz in streams], g1_ref, w1_ref, w3_ref, w2_ref, ff_chunk)
    x1 = x1[0] if n_streams == 1 else jnp.concatenate(x1, axis=0)
    x1_ref[...] = x1

    h = _rms(x1, gmix_ref[...]).astype(BF16)
    z = _dot(h, win_ref[...])

    row = lax.broadcasted_iota(jnp.int32, (tm, HEAD_PAD), 0) + tile * tm
    pos = (row & (seq_len - 1)) + pos_base
    ang = pos.astype(F32) * freq_ref[...]
    cos = jnp.cos(ang) * sign_ref[1:2, :]
    sin = jnp.sin(ang) * sign_ref[0:1, :]

    def rope(blk, c, s):
        return blk * c + pltpu.roll(blk, ROPE_PARTNER_ROLL, axis=1) * s

    cqn = _rms(z[:, ZC_Q:ZC_Q + Q_LORA], gq_ref[...]).astype(BF16)
    qz = _dot(cqn, wuq_ref[...])
    cos_q = cos * Q_SCALE
    sin_q = sin * Q_SCALE
    for hd in range(MLA_HEADS):
        lo = hd * HEAD_PAD
        q_ref[:, lo:lo + HEAD_PAD] = rope(qz[:, lo:lo + HEAD_PAD], cos_q, sin_q).astype(BF16)

    ckv = _rms(z[:, ZC_KV:ZC_KV + KV_LORA], gkv_ref[...])
    ckv_ref[...] = ckv
    kpe_blk = rope(z[:, ZC_KPE:ZC_KPE + HEAD_PAD], cos, sin)
    kpe_ref[...] = kpe_blk[:, QK_NOPE:QK_NOPE + QK_ROPE]

    ckv_b = ckv.astype(BF16)
    kz = _dot(ckv_b, wukv_ref[:, :MLA_PAD])
    for hd in range(MLA_HEADS):
        lo = hd * HEAD_PAD
        k_ref[:, lo:lo + HEAD_PAD] = (kz[:, lo:lo + HEAD_PAD] + kpe_blk).astype(BF16)
    if v_transposed:
        vt = _dot_nt(wuvt_ref[...], ckv_b)
        ridx = lax.broadcasted_iota(jnp.int32, vt.shape, 0)
        is_one = ridx == V_HEAD
        for hd in range(1, MLA_HEADS):
            is_one = is_one | (ridx == hd * VT_HEAD + V_HEAD)
        v_ref[...] = jnp.where(is_one, 1.0, vt).astype(BF16)
    else:
        v_ref[...] = _dot(ckv_b, wukv_ref[:, MLA_PAD:]).astype(BF16)

    xbr = z[:, ZC_XBR:ZC_XBR + LRU_WIDTH]
    gbr = z[:, ZC_GBR:ZC_GBR + LRU_WIDTH]
    if pipelined:
        zbuf_scr[:, :LRU_WIDTH] = xbr
        zbuf_scr[:, LRU_WIDTH:] = gbr
    elif seq_len >= tm:
        @pl.when(lax.rem(step, seq_len // tm) == 0)
        def _():
            ccar_scr[...] = cinit_ref[0]
            hcar_scr[...] = hinit_ref[0]

        out, tail, h_last = _lru_block(xbr, gbr, ccar_scr[...], hcar_scr[0:1, :], xe_scr, *lru_consts)
        ccar_scr[...] = tail
        hcar_scr[...] = bcast(h_last)
        lru_ref[...] = out
        cstate_ref[0] = tail
        hstate_ref[0] = bcast(h_last)
    else:
        for b in range(tm // seq_len):
            lo = b * seq_len
            out, tail, h_last = _lru_block(xbr[lo:lo + seq_len, :], gbr[lo:lo + seq_len, :], cinit_ref[b],
                                           hinit_ref[b, 0:1, :], xe_scr, *lru_consts)
            lru_ref[lo:lo + seq_len, :] = out
            cstate_ref[b] = tail
            hstate_ref[b] = bcast(h_last)


def _head_call(x, conv_init, h_init, wts, *, seq_len, pos_base, tm, ff_chunk, v_transposed):
    n = x.shape[0]
    nseq = n // seq_len
    assert n % tm == 0 and seq_len & (seq_len - 1) == 0 and seq_len >= SUBLANES
    assert seq_len % tm == 0 or tm % seq_len == 0
    nsub = max(1, tm // seq_len)
    n_tiles = n // tm
    pipelined = seq_len > tm
    if pipelined:
        row = lambda w: pl.BlockSpec((tm, w), lambda i: (jnp.minimum(i, n_tiles - 1), 0))
        lru_tile = lambda i: jnp.maximum(i - 1, 0)
        state = pl.BlockSpec((1, SUBLANES, LRU_WIDTH), lambda i: ((lru_tile(i) * tm) // seq_len, 0, 0))
    else:
        row = lambda w: pl.BlockSpec((tm, w), lambda i: (i, 0))
        state = pl.BlockSpec((nsub, SUBLANES, LRU_WIDTH), lambda i: ((i * tm) // (seq_len * nsub), 0, 0))
    consts = [wts['ffn1_norm'], wts['ffn1_w1'], wts['ffn1_w3'], wts['ffn1_w2'], wts['mix_norm'],
              wts['w_in_ext'], wts['q_norm'], wts['w_uq_ext'], wts['kv_norm'], wts['w_ukv_ext'],
              wts['w_uvt_ext'], wts['rope_freq'], wts['rope_sign'],
              wts['conv_w'], wts['conv_b'], wts['lru_wg'], wts['lru_bg'], wts['lru_lambda'],
              wts['lru_out_norm']]
    v_shape = (VT_ROWS, n) if v_transposed else (n, MLA_PAD)
    if v_transposed:
        v_spec = pl.BlockSpec((VT_ROWS, tm), lambda i: (0, jnp.minimum(i, n_tiles - 1)))
    else:
        v_spec = row(MLA_PAD)
    state_shape = jax.ShapeDtypeStruct((nseq, SUBLANES, LRU_WIDTH), F32)
    out_shapes = [
        jax.ShapeDtypeStruct((n, D_MODEL), F32),
        jax.ShapeDtypeStruct((n, MLA_PAD), BF16),
        jax.ShapeDtypeStruct((n, MLA_PAD), BF16),
        jax.ShapeDtypeStruct(v_shape, BF16),
        jax.ShapeDtypeStruct((n, KV_LORA), F32),
        jax.ShapeDtypeStruct((n, QK_ROPE), F32),
        jax.ShapeDtypeStruct((n, LRU_WIDTH), BF16),
        state_shape,
        state_shape,
    ]
    out_specs = [row(s.shape[1]) for s in out_shapes[:7]] + [state, state]
    out_specs[3] = v_spec
    if pipelined:
        out_specs[6] = pl.BlockSpec((tm, LRU_WIDTH), lambda i: (lru_tile(i), 0))
    lru_rows = min(tm, seq_len)
    return pl.pallas_call(
        functools.partial(_head_kernel, tm=tm, seq_len=seq_len, pos_base=pos_base, ff_chunk=ff_chunk,
                          v_transposed=v_transposed, n_tiles=n_tiles, pipelined=pipelined,
                          n_streams=ROW_STREAMS if tm % (ROW_STREAMS * 16) == 0 and tm >= 512 else 1),
        grid=(n_tiles + 1 if pipelined else n_tiles,),
        in_specs=[row(D_MODEL), state, state] + [_const_spec(c.shape) for c in consts],
        out_specs=out_specs,
        out_shape=out_shapes,
        scratch_shapes=[pltpu.VMEM((lru_rows + SUBLANES, LRU_WIDTH), F32),
                        pltpu.VMEM((SUBLANES, LRU_WIDTH), F32),
                        pltpu.VMEM((SUBLANES, LRU_WIDTH), F32),
                        pltpu.VMEM((tm if pipelined else SUBLANES, 2 * LRU_WIDTH), F32)],
        compiler_params=pltpu.CompilerParams(dimension_semantics=("arbitrary",),
                                             vmem_limit_bytes=VMEM_LIMIT_BYTES),
        name="head",
    )(x, conv_init, h_init, *consts)


def _attn_finalize(o_heads, gattn_ref, out_ref):
    ss = o_heads[0] * o_heads[0]
    for o in o_heads[1:]:
        ss = ss + o * o
    ms = jnp.sum(ss, axis=-1, keepdims=True) * (1.0 / MLA_WIDTH)
    inv = lax.rsqrt(ms + EPS)
    for hd, o in enumerate(o_heads):
        lo = hd * HEAD_PAD
        out_ref[:, lo:lo + HEAD_PAD] = (o * inv * gattn_ref[:, lo:lo + HEAD_PAD]).astype(BF16)


def _mla_prompt_step(q_ref, k_ref, vt_ref, m_scr, mx_scr, acc_scr, cur, first, visible, *, rescale,
                     keys=None, queries=None, second_part=False):
    k0, k1 = keys if keys is not None else (0, k_ref.shape[0])
    q0, q1 = queries if queries is not None else (0, q_ref.shape[0])

    def scores(hd):
        lo = hd * HEAD_PAD
        return _dot_nt(k_ref[k0:k1, lo:lo + HEAD_PAD], q_ref[q0:q1, lo:lo + HEAD_PAD])

    pending = [scores(hd) for hd in range(QK_LOOKAHEAD)]
    for hd in range(MLA_HEADS):
        st = pending[hd]
        if hd + QK_LOOKAHEAD < MLA_HEADS:
            pending.append(scores(hd + QK_LOOKAHEAD))
        if visible is not None:
            st = jnp.where(visible, st, NEG_BIG)
        vo = hd * VT_HEAD
        blk_max = jnp.max(st, axis=0, keepdims=True)
        if rescale:
            m_prev = jnp.where(first, NEG_BIG, m_scr[hd:hd + 1, q0:q1])
            m_new = jnp.maximum(m_prev, blk_max)
            alpha = jnp.exp2(m_prev - m_new)
            pt = jnp.exp2(st - m_new).astype(BF16)
            acc_scr[cur, vo:vo + VT_HEAD, q0:q1] = (alpha * acc_scr[cur, vo:vo + VT_HEAD, q0:q1]
                                                    + _dot(vt_ref[vo:vo + VT_HEAD, k0:k1], pt))
            m_scr[hd:hd + 1, q0:q1] = m_new
        else:
            src = 1 - cur if second_part else cur
            mx_scr[hd:hd + 1, q0:q1] = jnp.maximum(mx_scr[hd:hd + 1, q0:q1], blk_max) if second_part else blk_max
            pt = jnp.exp2(st - m_scr[hd:hd + 1, q0:q1]).astype(BF16)
            acc_scr[1 - cur, vo:vo + VT_HEAD, q0:q1] = (acc_scr[src, vo:vo + VT_HEAD, q0:q1]
                                                        + _dot(vt_ref[vo:vo + VT_HEAD, k0:k1], pt))


def _mla_prompt_kernel(qi_ref, ki_ref, q_ref, k_ref, vt_ref, gattn_ref, out_ref,
                       m_scr, mx_scr, acc_scr, cur_scr, *, tq, tk):
    step = pl.program_id(0)
    q_lo = qi_ref[step] * tq
    k_lo = ki_ref[step] * tk
    first = k_lo == 0

    @pl.when(first)
    def _():
        m_scr[...] = jnp.zeros(m_scr.shape, F32)
        acc_scr[0] = jnp.zeros(acc_scr.shape[1:], F32)
        cur_scr[0] = 0

    cur = cur_scr[0]
    fully_visible = k_lo + tk <= q_lo + CHUNK

    def chunk_mask():
        kc = (lax.broadcasted_iota(jnp.int32, (tk, tq), 0) + k_lo) // CHUNK
        qc = (lax.broadcasted_iota(jnp.int32, (tk, tq), 1) + q_lo) // CHUNK
        return kc <= qc

    step_fn = functools.partial(_mla_prompt_step, q_ref, k_ref, vt_ref, m_scr, mx_scr, acc_scr, cur, first)

    @pl.when(fully_visible)
    def _():
        step_fn(None, rescale=False)

    @pl.when(jnp.logical_not(fully_visible))
    def _():
        if tq == tk:
            vis = chunk_mask()
            step_fn(vis[:tk // 2, :], rescale=False, keys=(0, tk // 2))
            step_fn(vis[tk // 2:, tq // 2:], rescale=False, keys=(tk // 2, tk), queries=(tq // 2, tq),
                    second_part=True)
        else:
            step_fn(chunk_mask(), rescale=False)

    excess = mx_scr[...] - m_scr[...]
    redo = (jnp.max(excess) > RESCALE_THRESHOLD) | (first & (jnp.min(excess) < -FIRST_BLOCK_FLOOR))

    @pl.when(redo)
    def _():
        step_fn(chunk_mask(), rescale=True)

    @pl.when(jnp.logical_not(redo))
    def _():
        cur_scr[0] = 1 - cur

    @pl.when(k_lo + tk >= q_lo + tq)
    def _():
        fin = cur_scr[0]
        parts = []
        for hd in range(MLA_HEADS):
            vo = hd * VT_HEAD
            parts.append(acc_scr[fin, vo:vo + V_HEAD, :] / acc_scr[fin, vo + V_HEAD:vo + V_HEAD + 1, :])
        o = jnp.concatenate(parts, axis=0).T
        out_ref[...] = _rms(o, gattn_ref[...]).astype(BF16)


def _mla_prompt_call(q, k, vt, gattn, *, tq, tk):
    n = q.shape[0]
    assert n % tq == 0 and tq % tk == 0 and tk % CHUNK == 0
    pairs = [(qi, ki) for qi in range(n // tq) for ki in range((qi + 1) * tq // tk)]
    qi_tab = jnp.asarray([p[0] for p in pairs], jnp.int32)
    ki_tab = jnp.asarray([p[1] for p in pairs], jnp.int32)
    grid_spec = pltpu.PrefetchScalarGridSpec(
        num_scalar_prefetch=2,
        grid=(len(pairs),),
        in_specs=[pl.BlockSpec((tq, MLA_PAD), lambda s, qi, ki: (qi[s], 0)),
                  pl.BlockSpec((tk, MLA_PAD), lambda s, qi, ki: (ki[s], 0)),
                  pl.BlockSpec((VT_ROWS, tk), lambda s, qi, ki: (0, ki[s])),
                  pl.BlockSpec(gattn.shape, lambda s, qi, ki: (0, 0))],
        out_specs=pl.BlockSpec((tq, MLA_WIDTH), lambda s, qi, ki: (qi[s], 0)),
        scratch_shapes=[pltpu.VMEM((MLA_HEADS, tq), F32),
                        pltpu.VMEM((MLA_HEADS, tq), F32),
                        pltpu.VMEM((2, VT_ROWS, tq), F32),
                        pltpu.SMEM((1,), jnp.int32)],
    )
    return pl.pallas_call(
        functools.partial(_mla_prompt_kernel, tq=tq, tk=tk),
        grid_spec=grid_spec,
        out_shape=jax.ShapeDtypeStruct((n, MLA_WIDTH), BF16),
        compiler_params=pltpu.CompilerParams(dimension_semantics=("arbitrary",),
                                             vmem_limit_bytes=VMEM_LIMIT_BYTES),
        name="mla_prompt",
    )(qi_tab, ki_tab, q, k, vt, gattn)


def _mla_sample_kernel(q_ref, kn_ref, vn_ref, ckv_ref, kpe_ref, wukv_ref, gattn_ref, out_ref):
    t = q_ref.shape[0]
    heads = [slice(hd * HEAD_PAD, (hd + 1) * HEAD_PAD) for hd in range(MLA_HEADS)]
    cp = ckv_ref[0].astype(BF16)
    past = cp.shape[0]
    kpe = jnp.concatenate([jnp.zeros((past, QK_NOPE), F32), kpe_ref[0],
                           jnp.zeros((past, HEAD_PAD - QK_NOPE - QK_ROPE), F32)], axis=1).astype(BF16)
    q_all = jnp.concatenate([q_ref[:, hs] for hs in heads], axis=0)
    q_lat = jnp.concatenate([_dot_nt(q_ref[:, hs], wukv_ref[:, hs]) for hs in heads], axis=0)
    s1 = _dot_nt(q_lat.astype(BF16), cp) + _dot_nt(q_all, kpe)
    s2 = jnp.concatenate([_dot_nt(q_ref[:, hs], kn_ref[:, hs]) for hs in heads], axis=0)
    m = jnp.maximum(jnp.max(s1, axis=-1, keepdims=True), jnp.max(s2, axis=-1, keepdims=True))
    p1 = jnp.exp2(s1 - m)
    p2 = jnp.exp2(s2 - m)
    l = jnp.sum(p1, axis=-1, keepdims=True) + jnp.sum(p2, axis=-1, keepdims=True)
    p2 = p2.astype(BF16)
    o_lat = _dot(p1.astype(BF16), cp).astype(BF16)
    o_heads = []
    for hd, hs in enumerate(heads):
        rows = slice(hd * t, (hd + 1) * t)
        w_uv = wukv_ref[:, MLA_PAD + hd * HEAD_PAD:MLA_PAD + (hd + 1) * HEAD_PAD]
        o = _dot(o_lat[rows, :], w_uv) + _dot(p2[rows, :], vn_ref[:, hs])
        o_heads.append(o / l[rows, :])
    _attn_finalize(o_heads, gattn_ref, out_ref)


def _mla_sample_call(q, k, v, ckv_past, kpe_past, w_ukv_ext, gattn, *, t_new):
    n = q.shape[0]
    nbatch, past, _ = ckv_past.shape
    assert n == nbatch * t_new
    rows = pl.BlockSpec((t_new, MLA_PAD), lambda b: (b, 0))
    return pl.pallas_call(
        _mla_sample_kernel,
        grid=(nbatch,),
        in_specs=[rows, rows, rows,
                  pl.BlockSpec((1, past, KV_LORA), lambda b: (b, 0, 0)),
                  pl.BlockSpec((1, past, QK_ROPE), lambda b: (b, 0, 0)),
                  _const_spec(w_ukv_ext.shape), _const_spec(gattn.shape)],
        out_specs=rows,
        out_shape=jax.ShapeDtypeStruct((n, MLA_PAD), BF16),
        compiler_params=pltpu.CompilerParams(dimension_semantics=("arbitrary",),
                                             vmem_limit_bytes=VMEM_LIMIT_BYTES),
        name="mla_sample",
    )(q, k, v, ckv_past, kpe_past, w_ukv_ext, gattn)


def _lru_block(x, gate, conv_prev, h_prev, xe_scr, convw_ref, convb_ref, wg_ref, bg_ref, lam_ref, gl_ref):
    tm = x.shape[0]
    xe_scr[0:SUBLANES, :] = conv_prev
    xe_scr[SUBLANES:SUBLANES + tm, :] = x
    xc = convb_ref[...] + xe_scr[pl.ds(SUBLANES - 3, tm), :] * convw_ref[0:1, :]
    xc = xc + xe_scr[pl.ds(SUBLANES - 2, tm), :] * convw_ref[1:2, :]
    xc = xc + xe_scr[pl.ds(SUBLANES - 1, tm), :] * convw_ref[2:3, :]
    xc = xc + x * convw_ref[3:4, :]
    tail = x[tm - SUBLANES:, :]

    y = _dot(xc.astype(BF16), wg_ref[...]) + bg_ref[...]
    r = jax.nn.sigmoid(y[:, :LRU_WIDTH])
    ig = jax.nn.sigmoid(y[:, LRU_WIDTH:])
    nl = -lam_ref[...]
    softplus = jnp.maximum(nl, 0.0) + jnp.log1p(jnp.exp(-jnp.abs(nl)))
    log_a = -LRU_C * r * softplus
    a = jnp.exp(log_a)
    th = jnp.tanh(log_a)
    one_minus_a2 = -2.0 * th / (1.0 - th)
    b = jnp.sqrt(one_minus_a2) * ig * xc

    sub = lax.broadcasted_iota(jnp.int32, (tm, LRU_WIDTH), 0) & (SUBLANES - 1)
    s = 1
    while s < SUBLANES:
        valid = sub >= s
        a_sh = pltpu.roll(a, s, axis=0)
        b_sh = pltpu.roll(b, s, axis=0)
        b = jnp.where(valid, a * b_sh + b, b)
        a = jnp.where(valid, a * a_sh, a)
        s *= 2
    h_groups = []
    for g in range(tm // SUBLANES):
        lo = g * SUBLANES
        hg = b[lo:lo + SUBLANES, :] + a[lo:lo + SUBLANES, :] * h_prev
        h_groups.append(hg)
        h_prev = hg[SUBLANES - 1:SUBLANES, :]
    h = jnp.concatenate(h_groups, axis=0)
    out = _rms(jax.nn.gelu(gate) * h, gl_ref[...]).astype(BF16)
    return out, tail, h_prev


def _tail_kernel(x1_ref, attn_ref, lru_ref, woa_ref, wol_ref, gx_ref, wmq_ref, mk_ref, mv_ref, wmo_ref,
                 g2_ref, w1_ref, w3_ref, w2_ref, gf_ref, y_ref, *, tm, nb, ff_chunk, n_streams):
    streams = _row_streams(tm, n_streams)
    x2s = [x1_ref[lo:lo + sz, :] + _dot(attn_ref[lo:lo + sz, :], woa_ref[...])
           + _dot(lru_ref[lo:lo + sz, :], wol_ref[...]) for lo, sz in streams]

    hqs = [_rms(x2, gx_ref[...]).astype(BF16) for x2 in x2s]
    qs = [(_dot(hq, wmq_ref[...]) * (MEM_HEAD_DIM ** -0.5)).astype(BF16) for hq in hqs]

    def cross_attend(q, mem_batch):
        o_heads = []
        for hd in range(MEM_HEADS):
            lo = hd * MEM_HEAD_DIM
            s = _dot_nt(q[:, lo:lo + MEM_HEAD_DIM], mk_ref[mem_batch, :, lo:lo + MEM_HEAD_DIM])
            p = jnp.exp(s - jnp.max(s, axis=-1, keepdims=True))
            l = jnp.sum(p, axis=-1, keepdims=True)
            o_heads.append(_dot(p.astype(BF16), mv_ref[mem_batch, :, lo:lo + MEM_HEAD_DIM]) / l)
        return jnp.concatenate(o_heads, axis=1)

    if nb == 1:
        os_ = [cross_attend(q, 0) for q in qs]
    else:
        rows = tm // nb
        os_ = [jnp.concatenate([cross_attend(qs[0][b * rows:(b + 1) * rows, :], b) for b in range(nb)], axis=0)]
    x3s = [x2 + _dot(o.astype(BF16), wmo_ref[...]) for x2, o in zip(x2s, os_)]

    x4s = _swiglu_half(x3s, g2_ref, w1_ref, w3_ref, w2_ref, ff_chunk)
    for (lo, sz), x4 in zip(streams, x4s):
        y_ref[lo:lo + sz, :] = _rms(x4, gf_ref[...])


def _tail_call(x1, attn, lru, mk, mv, wts, *, seq_len, tm, nb, ff_chunk):
    n = x1.shape[0]
    assert n % tm == 0 and tm % nb == 0 and mk.shape[0] * seq_len == n
    assert (nb == 1 and seq_len % tm == 0) or nb * seq_len == tm
    row = lambda w: pl.BlockSpec((tm, w), lambda i: (i, 0))
    mem = pl.BlockSpec((nb, N_MEM, D_MODEL), lambda i: ((i * tm) // (seq_len * nb), 0, 0))
    c = lambda name: _const_spec(wts[name].shape)
    w_out_attn = wts['w_out_attn'] if attn.shape[1] == MLA_WIDTH else wts['w_out_attn_pad']
    return pl.pallas_call(
        functools.partial(_tail_kernel, tm=tm, nb=nb, ff_chunk=ff_chunk,
                          n_streams=ROW_STREAMS if nb == 1 and tm % (ROW_STREAMS * 16) == 0 else 1),
        grid=(n // tm,),
        in_specs=[row(D_MODEL), row(attn.shape[1]), row(LRU_WIDTH), _const_spec(w_out_attn.shape),
                  c('w_out_lru'), c('xattn_norm'), c('w_mq'),
                  mem, mem, c('w_mo'), c('ffn2_norm'), c('ffn2_w1'), c('ffn2_w3'), c('ffn2_w2'),
                  c('final_norm')],
        out_specs=row(D_MODEL),
        out_shape=jax.ShapeDtypeStruct((n, D_MODEL), F32),
        compiler_params=pltpu.CompilerParams(dimension_semantics=("arbitrary",),
                                             vmem_limit_bytes=VMEM_LIMIT_BYTES),
        name="tail",
    )(x1, attn, lru, w_out_attn, wts['w_out_lru'], wts['xattn_norm'], wts['w_mq'], mk, mv, wts['w_mo'],
      wts['ffn2_norm'], wts['ffn2_w1'], wts['ffn2_w3'], wts['ffn2_w2'], wts['final_norm'])


def _memkv_kernel(mem_ref, g_ref, wmk_ref, wmv_ref, k_ref, v_ref, kb_ref, vb_ref):
    m = _rms(mem_ref[...], g_ref[...]).astype(BF16)
    k = _dot(m, wmk_ref[...])
    v = _dot(m, wmv_ref[...])
    for hd in range(MEM_HEADS):
        lo = hd * MEM_HEAD_DIM
        k_ref[:, hd, :] = k[:, lo:lo + MEM_HEAD_DIM]
        v_ref[:, hd, :] = v[:, lo:lo + MEM_HEAD_DIM]
    kb_ref[...] = k.astype(BF16)
    vb_ref[...] = v.astype(BF16)


def _memkv_call(mem, wts):
    n = mem.shape[0]
    spec = pl.BlockSpec((n, D_MODEL), lambda i: (0, 0))
    cache_spec = pl.BlockSpec((n, MEM_HEADS, MEM_HEAD_DIM), lambda i: (0, 0, 0))
    cache = jax.ShapeDtypeStruct((n, MEM_HEADS, MEM_HEAD_DIM), F32)
    token_major = jax.ShapeDtypeStruct((n, D_MODEL), BF16)
    return pl.pallas_call(
        _memkv_kernel,
        grid=(1,),
        in_specs=[spec, _const_spec(wts['mem_norm'].shape), _const_spec(wts['w_mk'].shape),
                  _const_spec(wts['w_mv'].shape)],
        out_specs=[cache_spec, cache_spec, spec, spec],
        out_shape=[cache, cache, token_major, token_major],
        compiler_params=pltpu.CompilerParams(dimension_semantics=("arbitrary",),
                                             vmem_limit_bytes=VMEM_LIMIT_BYTES),
        name="memkv",
    )(mem, wts['mem_norm'], wts['w_mk'], wts['w_mv'])


CAST_STEPS = 8


def _cast_kernel(*refs):
    n = len(refs) // 2
    for w_ref, o_ref in zip(refs[:n], refs[n:]):
        o_ref[...] = w_ref[...].astype(BF16)


def _cast_bf16_call(ws):
    specs = []
    for w in ws:
        rows, cols = w.shape
        assert rows % (CAST_STEPS * 16) == 0 and cols % LANES == 0
        specs.append(pl.BlockSpec((rows // CAST_STEPS, cols), lambda i: (i, 0)))
    return pl.pallas_call(
        _cast_kernel,
        grid=(CAST_STEPS,),
        in_specs=specs,
        out_specs=specs,
        out_shape=[jax.ShapeDtypeStruct(w.shape, BF16) for w in ws],
        compiler_params=pltpu.CompilerParams(dimension_semantics=("arbitrary",),
                                             vmem_limit_bytes=VMEM_LIMIT_BYTES),
        name="cast_bf16",
    )(*ws)


def _prep_weights(p):
    w = {}
    rowv = lambda a: a.reshape(1, -1).astype(F32)
    for name in ('ffn1_norm', 'mix_norm', 'q_norm', 'kv_norm', 'xattn_norm', 'ffn2_norm', 'final_norm',
                 'mem_norm', 'lru_out_norm', 'conv_b', 'lru_lambda'):
        w[name] = rowv(p[name])
    big = ('ffn1_w1', 'ffn1_w3', 'ffn1_w2', 'ffn2_w1', 'ffn2_w3', 'ffn2_w2', 'w_mq', 'w_mk', 'w_mv', 'w_mo')
    w.update(zip(big, _cast_bf16_call([p[name] for name in big])))

    half = QK_ROPE // 2
    w_in = p['w_in']
    kpe_cols = w_in[:, ZC_KPE:ZC_KPE + QK_ROPE]
    assert HEAD_PAD == QK_NOPE + 2 * QK_ROPE
    pad_l = jnp.zeros((D_MODEL, QK_NOPE), F32)
    kpe_blk = jnp.concatenate([pad_l, kpe_cols, kpe_cols[:, half:], kpe_cols[:, :half]], axis=1)
    lru_cols = w_in[:, Q_LORA + KV_LORA + QK_ROPE:]
    w['w_in_ext'] = jnp.concatenate([w_in[:, :Q_LORA + KV_LORA], kpe_blk, lru_cols], axis=1).astype(BF16)

    wq = p['w_uq'].reshape(Q_LORA, MLA_HEADS, QK_NOPE + QK_ROPE)
    q_blk = jnp.concatenate([wq, wq[:, :, QK_NOPE + half:], wq[:, :, QK_NOPE:QK_NOPE + half]], axis=2)
    w['w_uq_ext'] = q_blk.reshape(Q_LORA, MLA_PAD).astype(BF16)

    wkv = p['w_ukv'].reshape(KV_LORA, MLA_HEADS, QK_NOPE + V_HEAD)
    zk = jnp.zeros((KV_LORA, MLA_HEADS, HEAD_PAD - QK_NOPE), F32)
    zv = jnp.zeros((KV_LORA, MLA_HEADS, HEAD_PAD - V_HEAD), F32)
    k_part = jnp.concatenate([wkv[:, :, :QK_NOPE], zk], axis=2).reshape(KV_LORA, MLA_PAD)
    v_part = jnp.concatenate([wkv[:, :, QK_NOPE:], zv], axis=2).reshape(KV_LORA, MLA_PAD)
    w['w_ukv_ext'] = jnp.concatenate([k_part, v_part], axis=1).astype(BF16)
    vt = jnp.concatenate([wkv[:, :, QK_NOPE:], jnp.zeros((KV_LORA, MLA_HEADS, VT_HEAD - V_HEAD), F32)], axis=2)
    w['w_uvt_ext'] = vt.reshape(KV_LORA, VT_ROWS).T.astype(BF16)

    inv_freq = ROPE_THETA ** (-jnp.arange(0, QK_ROPE, 2, dtype=F32) / QK_ROPE)
    zl = jnp.zeros((QK_NOPE,), F32)
    zr = jnp.zeros((HEAD_PAD - QK_NOPE - QK_ROPE,), F32)
    w['rope_freq'] = jnp.concatenate([zl, inv_freq, inv_freq, zr]).reshape(1, HEAD_PAD)
    sign = jnp.concatenate([zl, -jnp.ones((half,), F32), jnp.ones((half,), F32), zr])
    keep = jnp.concatenate([jnp.ones((QK_NOPE + QK_ROPE,), F32), zr])
    w['rope_sign'] = jnp.stack([sign, keep])

    g_attn = jnp.concatenate([p['attn_out_norm'].reshape(MLA_HEADS, V_HEAD),
                              jnp.zeros((MLA_HEADS, HEAD_PAD - V_HEAD), F32)], axis=1)
    w['attn_out_norm_pad'] = g_attn.reshape(1, MLA_PAD)
    w['attn_out_norm'] = rowv(p['attn_out_norm'])

    w_out = p['w_out']
    wo_attn = jnp.concatenate([w_out[:MLA_WIDTH].reshape(MLA_HEADS, V_HEAD, D_MODEL),
                               jnp.zeros((MLA_HEADS, HEAD_PAD - V_HEAD, D_MODEL), F32)], axis=1)
    w['w_out_attn_pad'] = wo_attn.reshape(MLA_PAD, D_MODEL).astype(BF16)
    w['w_out_attn'] = w_out[:MLA_WIDTH].astype(BF16)
    w['w_out_lru'] = w_out[MLA_WIDTH:].astype(BF16)

    def block_diag(wb):
        eye = jnp.eye(LRU_BLOCKS, dtype=F32)
        return (eye[:, None, :, None] * wb[:, :, None, :]).reshape(LRU_WIDTH, LRU_WIDTH)

    w['lru_wg'] = jnp.concatenate([block_diag(p['lru_wa']), block_diag(p['lru_wx'])], axis=1).astype(BF16)
    w['lru_bg'] = jnp.concatenate([p['lru_ba'].reshape(1, -1), p['lru_bx'].reshape(1, -1)], axis=1).astype(F32)
    w['conv_w'] = p['conv_w'].astype(F32)
    return w


def _pick_tile(n, cap):
    t = min(n, cap)
    while n % t:
        t //= 2
    return t


def _group(x, mk, mv, wts, *, past, conv_init, h_init):
    nbatch, seq, _ = x.shape
    n = nbatch * seq
    xf = x.reshape(n, D_MODEL)
    tm = _pick_tile(n, 512)
    ff_chunk = D_FF // 2
    pos_base = 0 if past is None else past[0].shape[1]
    x1, q, k, v, ckv, kpe, lru, cstate, hstate = _head_call(
        xf, conv_init, h_init, wts, seq_len=seq, pos_base=pos_base, tm=tm, ff_chunk=ff_chunk,
        v_transposed=past is None)
    if past is None:
        assert nbatch == 1
        tq = _pick_tile(n, 1024)
        attn = _mla_prompt_call(q, k, v, wts['attn_out_norm'], tq=tq, tk=tq)
    else:
        ckv_past, kpe_past = past
        attn = _mla_sample_call(q, k, v, ckv_past, kpe_past, wts['w_ukv_ext'], wts['attn_out_norm_pad'],
                                t_new=seq)
    nb = tm // seq if seq < tm else 1
    y = _tail_call(x1, attn, lru, mk, mv, wts, seq_len=seq, tm=tm, nb=nb, ff_chunk=ff_chunk)
    return (y.reshape(nbatch, seq, D_MODEL), ckv.reshape(nbatch, seq, KV_LORA),
            kpe.reshape(nbatch, seq, QK_ROPE), cstate[:, SUBLANES - (CONV_W - 1):, :], hstate[:, 0, :])


def kernel(x_prompt, x_sample, mem_prompt, cache_mla_ckv, cache_mla_kpe, state_conv, state_lru, cache_mem_k, cache_mem_v, ffn1_norm, ffn1_w1, ffn1_w3, ffn1_w2, mix_norm, w_in, q_norm, w_uq, kv_norm, w_ukv, conv_w, conv_b, lru_wa, lru_ba, lru_wx, lru_bx, lru_lambda, attn_out_norm, lru_out_norm, w_out, mem_norm, xattn_norm, w_mq, w_mk, w_mv, w_mo, ffn2_norm, ffn2_w1, ffn2_w3, ffn2_w2, final_norm):
    depth = ffn1_norm.shape[0]
    assert depth == 1
    params = dict(ffn1_norm=ffn1_norm[0], ffn1_w1=ffn1_w1[0], ffn1_w3=ffn1_w3[0], ffn1_w2=ffn1_w2[0],
                  mix_norm=mix_norm[0], w_in=w_in[0], q_norm=q_norm[0], w_uq=w_uq[0], kv_norm=kv_norm[0],
                  w_ukv=w_ukv[0], conv_w=conv_w[0], conv_b=conv_b[0], lru_wa=lru_wa[0], lru_ba=lru_ba[0],
                  lru_wx=lru_wx[0], lru_bx=lru_bx[0], lru_lambda=lru_lambda[0],
                  attn_out_norm=attn_out_norm[0], lru_out_norm=lru_out_norm[0], w_out=w_out[0],
                  mem_norm=mem_norm[0], xattn_norm=xattn_norm[0], w_mq=w_mq[0], w_mk=w_mk[0], w_mv=w_mv[0],
                  w_mo=w_mo[0], ffn2_norm=ffn2_norm[0], ffn2_w1=ffn2_w1[0], ffn2_w3=ffn2_w3[0],
                  ffn2_w2=ffn2_w2[0], final_norm=final_norm)
    wts = _prep_weights(params)

    bp = x_prompt.shape[0]
    bs = x_sample.shape[0]
    n_mem = mem_prompt.shape[1]

    assert bp == 1
    mk_f, mv_f, mk_b, mv_b = _memkv_call(mem_prompt.reshape(bp * n_mem, D_MODEL), wts)
    zero_state = jnp.zeros((bp, SUBLANES, LRU_WIDTH), F32)
    y_p, ckv_p, kpe_p, conv_p, lru_p = _group(
        x_prompt, mk_b.reshape(bp, n_mem, D_MODEL), mv_b.reshape(bp, n_mem, D_MODEL), wts,
        past=None, conv_init=zero_state, h_init=zero_state)

    conv_init = jnp.pad(state_conv[0], ((0, 0), (SUBLANES - (CONV_W - 1), 0), (0, 0)))
    h_init = jnp.broadcast_to(state_lru[0][:, None, :], (bs, SUBLANES, LRU_WIDTH))
    y_s, ckv_s, kpe_s, conv_s, lru_s = _group(
        x_sample, cache_mem_k[0].reshape(bs, n_mem, D_MODEL).astype(BF16),
        cache_mem_v[0].reshape(bs, n_mem, D_MODEL).astype(BF16), wts,
        past=(cache_mla_ckv[0], cache_mla_kpe[0]), conv_init=conv_init, h_init=h_init)

    mem_shape = (1, bp, n_mem, MEM_HEADS, MEM_HEAD_DIM)
    return (y_p, y_s, ckv_p[None], kpe_p[None], conv_p[None], lru_p[None],
            mk_f.reshape(mem_shape), mv_f.reshape(mem_shape),
            ckv_s[None], kpe_s[None], conv_s[None], lru_s[None])
```

```python
import functools

import jax
import jax.numpy as jnp
from jax import lax
from jax.experimental import pallas as pl
from jax.experimental.pallas import tpu as pltpu

F32 = jnp.float32
BF16 = jnp.bfloat16

D_MODEL = 1024
CHUNK = 64
N_MEM = 256
MLA_HEADS = 8
Q_LORA = 384
KV_LORA = 256
QK_NOPE = 64
QK_ROPE = 32
V_HEAD = 64
MLA_WIDTH = MLA_HEADS * V_HEAD
MLA_SCALE = (QK_NOPE + QK_ROPE) ** -0.5
ROPE_THETA = 10000.0
LRU_WIDTH = 512
LRU_BLOCKS = 8
LRU_BLOCK = LRU_WIDTH // LRU_BLOCKS
CONV_W = 4
LRU_C = 8.0
MEM_HEADS = 4
MEM_HEAD_DIM = D_MODEL // MEM_HEADS
D_FF = 2816
EPS = 1e-6

LANES = 128
SUBLANES = 8
HEAD_PAD = LANES
MLA_PAD = MLA_HEADS * HEAD_PAD
VT_HEAD = V_HEAD + 16
VT_ROWS = MLA_HEADS * VT_HEAD
Q_SCALE = MLA_SCALE * 1.4426950408889634
QK_LOOKAHEAD = 2
ROW_STREAMS = 2
RESCALE_THRESHOLD = 64.0
FIRST_BLOCK_FLOOR = 64.0
ZC_Q = 0
ZC_KV = ZC_Q + Q_LORA
ZC_KPE = ZC_KV + KV_LORA
ZC_XBR = ZC_KPE + HEAD_PAD
ZC_GBR = ZC_XBR + LRU_WIDTH
Z_COLS = ZC_GBR + LRU_WIDTH
ROPE_PARTNER_ROLL = HEAD_PAD - QK_ROPE
NEG_BIG = -1e30
VMEM_LIMIT_BYTES = 56 * 1024 * 1024


def _const_spec(shape):
    nd = len(shape)
    return pl.BlockSpec(shape, lambda *_: (0,) * nd, pipeline_mode=pl.Buffered(1))


def _rms(x, g):
    ms = jnp.mean(x * x, axis=-1, keepdims=True)
    return x * lax.rsqrt(ms + EPS) * g


def _dot(a, b):
    return jnp.dot(a, b, preferred_element_type=F32)


def _dot_nt(a, b):
    return lax.dot_general(a, b, (((1,), (1,)), ((), ())), preferred_element_type=F32)


def _swiglu_half(xs, g_ref, w1_ref, w3_ref, w2_ref, ff_chunk):
    hs = [_rms(x, g_ref[...]).astype(BF16) for x in xs]
    accs = [jnp.zeros(x.shape, F32) for x in xs]
    for c in range(D_FF // ff_chunk):
        lo = c * ff_chunk
        ups = [(_dot(h, w1_ref[:, lo:lo + ff_chunk]), _dot(h, w3_ref[:, lo:lo + ff_chunk])) for h in hs]
        gs = [(a * jax.nn.sigmoid(a) * b).astype(BF16) for a, b in ups]
        accs = [acc + _dot(g, w2_ref[lo:lo + ff_chunk, :]) for acc, g in zip(accs, gs)]
    return [x + 0.5 * acc for x, acc in zip(xs, accs)]


def _row_streams(tm, n_streams):
    size = tm // n_streams
    return [(s * size, size) for s in range(n_streams)]


def _head_kernel(x_ref, cinit_ref, hinit_ref, g1_ref, w1_ref, w3_ref, w2_ref, gmix_ref, win_ref, gq_ref,
                 wuq_ref, gkv_ref, wukv_ref, wuvt_ref, freq_ref, sign_ref,
                 convw_ref, convb_ref, wg_ref, bg_ref, lam_ref, gl_ref,
                 x1_ref, q_ref, k_ref, v_ref, ckv_ref, kpe_ref, lru_ref, cstate_ref, hstate_ref,
                 xe_scr, ccar_scr, hcar_scr, zbuf_scr,
                 *, tm, seq_len, pos_base, ff_chunk, v_transposed, n_tiles, pipelined, n_streams):
    step = pl.program_id(0)
    lru_consts = (convw_ref, convb_ref, wg_ref, bg_ref, lam_ref, gl_ref)
    bcast = lambda hrow: jnp.broadcast_to(hrow, (SUBLANES, LRU_WIDTH))
    if pipelined:
        @pl.when(step == 0)
        def _():
            zbuf_scr[...] = jnp.zeros(zbuf_scr.shape, F32)
            ccar_scr[...] = jnp.zeros(ccar_scr.shape, F32)
            hcar_scr[...] = jnp.zeros(hcar_scr.shape, F32)

        new_seq = lax.rem(step - 1, seq_len // tm) == 0
        conv_prev = jnp.where(new_seq, cinit_ref[0], ccar_scr[...])
        h_prev = jnp.where(new_seq, hinit_ref[0, 0:1, :], hcar_scr[0:1, :])
        out, tail, h_last = _lru_block(zbuf_scr[:, :LRU_WIDTH], zbuf_scr[:, LRU_WIDTH:], conv_prev, h_prev,
                                       xe_scr, *lru_consts)
        lru_ref[...] = out
        ccar_scr[...] = tail
        hcar_scr[...] = bcast(h_last)
        cstate_ref[0] = tail
        hstate_ref[0] = bcast(h_last)
        tile = jnp.minimum(step, n_tiles - 1)
    else:
        tile = step

    streams = _row_streams(tm, n_streams)
    x1 = _swiglu_half([x_ref[lo:lo + sz, :] for lo, sz in streams], g1_ref, w1_ref, w3_ref, w2_ref, ff_chunk)
    x1 = x1[0] if n_streams == 1 else jnp.concatenate(x1, axis=0)
    x1_ref[...] = x1

    h = _rms(x1, gmix_ref[...]).astype(BF16)
    z = _dot(h, win_ref[...])

    row = lax.broadcasted_iota(jnp.int32, (tm, HEAD_PAD), 0) + tile * tm
    pos = (row & (seq_len - 1)) + pos_base
    ang = pos.astype(F32) * freq_ref[...]
    cos = jnp.cos(ang) * sign_ref[1:2, :]
    sin = jnp.sin(ang) * sign_ref[0:1, :]

    def rope(blk, c, s):
        return blk * c + pltpu.roll(blk, ROPE_PARTNER_ROLL, axis=1) * s

    cqn = _rms(z[:, ZC_Q:ZC_Q + Q_LORA], gq_ref[...]).astype(BF16)
    qz = _dot(cqn, wuq_ref[...])
    cos_q = cos * Q_SCALE
    sin_q = sin * Q_SCALE
    for hd in range(MLA_HEADS):
        lo = hd * HEAD_PAD
        q_ref[:, lo:lo + HEAD_PAD] = rope(qz[:, lo:lo + HEAD_PAD], cos_q, sin_q).astype(BF16)

    ckv = _rms(z[:, ZC_KV:ZC_KV + KV_LORA], gkv_ref[...])
    ckv_ref[...] = ckv
    kpe_blk = rope(z[:, ZC_KPE:ZC_KPE + HEAD_PAD], cos, sin)
    kpe_ref[...] = kpe_blk[:, QK_NOPE:QK_NOPE + QK_ROPE]

    ckv_b = ckv.astype(BF16)
    kz = _dot(ckv_b, wukv_ref[:, :MLA_PAD])
    for hd in range(MLA_HEADS):
        lo = hd * HEAD_PAD
        k_ref[:, lo:lo + HEAD_PAD] = (kz[:, lo:lo + HEAD_PAD] + kpe_blk).astype(BF16)
    if v_transposed:
        vt = _dot_nt(wuvt_ref[...], ckv_b)
        ridx = lax.broadcasted_iota(jnp.int32, vt.shape, 0)
        is_one = ridx == V_HEAD
        for hd in range(1, MLA_HEADS):
            is_one = is_one | (ridx == hd * VT_HEAD + V_HEAD)
        v_ref[...] = jnp.where(is_one, 1.0, vt).astype(BF16)
    else:
        v_ref[...] = _dot(ckv_b, wukv_ref[:, MLA_PAD:]).astype(BF16)

    xbr = z[:, ZC_XBR:ZC_XBR + LRU_WIDTH]
    gbr = z[:, ZC_GBR:ZC_GBR + LRU_WIDTH]
    if pipelined:
        zbuf_scr[:, :LRU_WIDTH] = xbr
        zbuf_scr[:, LRU_WIDTH:] = gbr
    elif seq_len >= tm:
        @pl.when(lax.rem(step, seq_len // tm) == 0)
        def _():
            ccar_scr[...] = cinit_ref[0]
            hcar_scr[...] = hinit_ref[0]

        out, tail, h_last = _lru_block(xbr, gbr, ccar_scr[...], hcar_scr[0:1, :], xe_scr, *lru_consts)
        ccar_scr[...] = tail
        hcar_scr[...] = bcast(h_last)
        lru_ref[...] = out
        cstate_ref[0] = tail
        hstate_ref[0] = bcast(h_last)
    else:
        for b in range(tm // seq_len):
            lo = b * seq_len
            out, tail, h_last = _lru_block(xbr[lo:lo + seq_len, :], gbr[lo:lo + seq_len, :], cinit_ref[b],
                                           hinit_ref[b, 0:1, :], xe_scr, *lru_consts)
            lru_ref[lo:lo + seq_len, :] = out
            cstate_ref[b] = tail
            hstate_ref[b] = bcast(h_last)


def _head_call(x, conv_init, h_init, wts, *, seq_len, pos_base, tm, ff_chunk, v_transposed):
    n = x.shape[0]
    nseq = n // seq_len
    assert n % tm == 0 and seq_len & (seq_len - 1) == 0 and seq_len >= SUBLANES
    assert seq_len % tm == 0 or tm % seq_len == 0
    nsub = max(1, tm // seq_len)
    n_tiles = n // tm
    pipelined = seq_len > tm
    if pipelined:
        row = lambda w: pl.BlockSpec((tm, w), lambda i: (jnp.minimum(i, n_tiles - 1), 0))
        lru_tile = lambda i: jnp.maximum(i - 1, 0)
        state = pl.BlockSpec((1, SUBLANES, LRU_WIDTH), lambda i: ((lru_tile(i) * tm) // seq_len, 0, 0))
    else:
        row = lambda w: pl.BlockSpec((tm, w), lambda i: (i, 0))
        state = pl.BlockSpec((nsub, SUBLANES, LRU_WIDTH), lambda i: ((i * tm) // (seq_len * nsub), 0, 0))
    consts = [wts['ffn1_norm'], wts['ffn1_w1'], wts['ffn1_w3'], wts['ffn1_w2'], wts['mix_norm'],
              wts['w_in_ext'], wts['q_norm'], wts['w_uq_ext'], wts['kv_norm'], wts['w_ukv_ext'],
              wts['w_uvt_ext'], wts['rope_freq'], wts['rope_sign'],
              wts['conv_w'], wts['conv_b'], wts['lru_wg'], wts['lru_bg'], wts['lru_lambda'],
              wts['lru_out_norm']]
    v_shape = (VT_ROWS, n) if v_transposed else (n, MLA_PAD)
    if v_transposed:
        v_spec = pl.BlockSpec((VT_ROWS, tm), lambda i: (0, jnp.minimum(i, n_tiles - 1)))
    else:
        v_spec = row(MLA_PAD)
    state_shape = jax.ShapeDtypeStruct((nseq, SUBLANES, LRU_WIDTH), F32)
    out_shapes = [
        jax.ShapeDtypeStruct((n, D_MODEL), F32),
        jax.ShapeDtypeStruct((n, MLA_PAD), BF16),
        jax.ShapeDtypeStruct((n, MLA_PAD), BF16),
        jax.ShapeDtypeStruct(v_shape, BF16),
        jax.ShapeDtypeStruct((n, KV_LORA), F32),
        jax.ShapeDtypeStruct((n, QK_ROPE), F32),
        jax.ShapeDtypeStruct((n, LRU_WIDTH), BF16),
        state_shape,
        state_shape,
    ]
    out_specs = [row(s.shape[1]) for s in out_shapes[:7]] + [state, state]
    out_specs[3] = v_spec
    if pipelined:
        out_specs[6] = pl.BlockSpec((tm, LRU_WIDTH), lambda i: (lru_tile(i), 0))
    lru_rows = min(tm, seq_len)
    return pl.pallas_call(
        functools.partial(_head_kernel, tm=tm, seq_len=seq_len, pos_base=pos_base, ff_chunk=ff_chunk,
                          v_transposed=v_transposed, n_tiles=n_tiles, pipelined=pipelined,
                          n_streams=ROW_STREAMS if tm % (ROW_STREAMS * 16) == 0 and tm >= 512 else 1),
        grid=(n_tiles + 1 if pipelined else n_tiles,),
        in_specs=[row(D_MODEL), state, state] + [_const_spec(c.shape) for c in consts],
        out_specs=out_specs,
        out_shape=out_shapes,
        scratch_shapes=[pltpu.VMEM((lru_rows + SUBLANES, LRU_WIDTH), F32),
                        pltpu.VMEM((SUBLANES, LRU_WIDTH), F32),
                        pltpu.VMEM((SUBLANES, LRU_WIDTH), F32),
                        pltpu.VMEM((tm if pipelined else SUBLANES, 2 * LRU_WIDTH), F32)],
        compiler_params=pltpu.CompilerParams(dimension_semantics=("arbitrary",),
                                             vmem_limit_bytes=VMEM_LIMIT_BYTES),
        name="head",
    )(x, conv_init, h_init, *consts)


def _attn_finalize(o_heads, gattn_ref, out_ref):
    ss = o_heads[0] * o_heads[0]
    for o in o_heads[1:]:
        ss = ss + o * o
    ms = jnp.sum(ss, axis=-1, keepdims=True) * (1.0 / MLA_WIDTH)
    inv = lax.rsqrt(ms + EPS)
    for hd, o in enumerate(o_heads):
        lo = hd * HEAD_PAD
        out_ref[:, lo:lo + HEAD_PAD] = (o * inv * gattn_ref[:, lo:lo + HEAD_PAD]).astype(BF16)


def _mla_prompt_step(q_ref, k_ref, vt_ref, m_scr, mx_scr, acc_scr, cur, first, visible, *, rescale,
                     keys=None, queries=None, second_part=False):
    k0, k1 = keys if keys is not None else (0, k_ref.shape[0])
    q0, q1 = queries if queries is not None else (0, q_ref.shape[0])

    def scores(hd):
        lo = hd * HEAD_PAD
        return _dot_nt(k_ref[k0:k1, lo:lo + HEAD_PAD], q_ref[q0:q1, lo:lo + HEAD_PAD])

    pending = [scores(hd) for hd in range(QK_LOOKAHEAD)]
    for hd in range(MLA_HEADS):
        st = pending[hd]
        if hd + QK_LOOKAHEAD < MLA_HEADS:
            pending.append(scores(hd + QK_LOOKAHEAD))
        if visible is not None:
            st = jnp.where(visible, st, NEG_BIG)
        vo = hd * VT_HEAD
        blk_max = jnp.max(st, axis=0, keepdims=True)
        if rescale:
            m_prev = jnp.where(first, NEG_BIG, m_scr[hd:hd + 1, q0:q1])
            m_new = jnp.maximum(m_prev, blk_max)
            alpha = jnp.exp2(m_prev - m_new)
            pt = jnp.exp2(st - m_new).astype(BF16)
            acc_scr[cur, vo:vo + VT_HEAD, q0:q1] = (alpha * acc_scr[cur, vo:vo + VT_HEAD, q0:q1]
                                                    + _dot(vt_ref[vo:vo + VT_HEAD, k0:k1], pt))
            m_scr[hd:hd + 1, q0:q1] = m_new
        else:
            src = 1 - cur if second_part else cur
            mx_scr[hd:hd + 1, q0:q1] = jnp.maximum(mx_scr[hd:hd + 1, q0:q1], blk_max) if second_part else blk_max
            pt = jnp.exp2(st - m_scr[hd:hd + 1, q0:q1]).astype(BF16)
            acc_scr[1 - cur, vo:vo + VT_HEAD, q0:q1] = (acc_scr[src, vo:vo + VT_HEAD, q0:q1]
                                                        + _dot(vt_ref[vo:vo + VT_HEAD, k0:k1], pt))


def _mla_prompt_kernel(qi_ref, ki_ref, q_ref, k_ref, vt_ref, gattn_ref, out_ref,
                       m_scr, mx_scr, acc_scr, cur_scr, *, tq, tk):
    step = pl.program_id(0)
    q_lo = qi_ref[step] * tq
    k_lo = ki_ref[step] * tk
    first = k_lo == 0

    @pl.when(first)
    def _():
        m_scr[...] = jnp.zeros(m_scr.shape, F32)
        acc_scr[0] = jnp.zeros(acc_scr.shape[1:], F32)
        cur_scr[0] = 0

    cur = cur_scr[0]
    fully_visible = k_lo + tk <= q_lo + CHUNK

    def chunk_mask():
        kc = (lax.broadcasted_iota(jnp.int32, (tk, tq), 0) + k_lo) // CHUNK
        qc = (lax.broadcasted_iota(jnp.int32, (tk, tq), 1) + q_lo) // CHUNK
        return kc <= qc

    step_fn = functools.partial(_mla_prompt_step, q_ref, k_ref, vt_ref, m_scr, mx_scr, acc_scr, cur, first)

    @pl.when(fully_visible)
    def _():
        step_fn(None, rescale=False)

    @pl.when(jnp.logical_not(fully_visible))
    def _():
        if tq == tk:
            vis = chunk_mask()
            step_fn(vis[:tk // 2, :], rescale=False, keys=(0, tk // 2))
            step_fn(vis[tk // 2:, tq // 2:], rescale=False, keys=(tk // 2, tk), queries=(tq // 2, tq),
                    second_part=True)
        else:
            step_fn(chunk_mask(), rescale=False)

    excess = mx_scr[...] - m_scr[...]
    redo = (jnp.max(excess) > RESCALE_THRESHOLD) | (first & (jnp.min(excess) < -FIRST_BLOCK_FLOOR))

    @pl.when(redo)
    def _():
        step_fn(chunk_mask(), rescale=True)

    @pl.when(jnp.logical_not(redo))
    def _():
        cur_scr[0] = 1 - cur

    @pl.when(k_lo + tk >= q_lo + tq)
    def _():
        fin = cur_scr[0]
        parts = []
        for hd in range(MLA_HEADS):
            vo = hd * VT_HEAD
            parts.append(acc_scr[fin, vo:vo + V_HEAD, :] / acc_scr[fin, vo + V_HEAD:vo + V_HEAD + 1, :])
        o = jnp.concatenate(parts, axis=0).T
        out_ref[...] = _rms(o, gattn_ref[...]).astype(BF16)


def _mla_prompt_call(q, k, vt, gattn, *, tq, tk):
    n = q.shape[0]
    assert n % tq == 0 and tq % tk == 0 and tk % CHUNK == 0
    pairs = [(qi, ki) for qi in range(n // tq) for ki in range((qi + 1) * tq // tk)]
    qi_tab = jnp.asarray([p[0] for p in pairs], jnp.int32)
    ki_tab = jnp.asarray([p[1] for p in pairs], jnp.int32)
    grid_spec = pltpu.PrefetchScalarGridSpec(
        num_scalar_prefetch=2,
        grid=(len(pairs),),
        in_specs=[pl.BlockSpec((tq, MLA_PAD), lambda s, qi, ki: (qi[s], 0)),
                  pl.BlockSpec((tk, MLA_PAD), lambda s, qi, ki: (ki[s], 0)),
                  pl.BlockSpec((VT_ROWS, tk), lambda s, qi, ki: (0, ki[s])),
                  pl.BlockSpec(gattn.shape, lambda s, qi, ki: (0, 0))],
        out_specs=pl.BlockSpec((tq, MLA_WIDTH), lambda s, qi, ki: (qi[s], 0)),
        scratch_shapes=[pltpu.VMEM((MLA_HEADS, tq), F32),
                        pltpu.VMEM((MLA_HEADS, tq), F32),
                        pltpu.VMEM((2, VT_ROWS, tq), F32),
                        pltpu.SMEM((1,), jnp.int32)],
    )
    return pl.pallas_call(
        functools.partial(_mla_prompt_kernel, tq=tq, tk=tk),
        grid_spec=grid_spec,
        out_shape=jax.ShapeDtypeStruct((n, MLA_WIDTH), BF16),
        compiler_params=pltpu.CompilerParams(dimension_semantics=("arbitrary",),
                                             vmem_limit_bytes=VMEM_LIMIT_BYTES),
        name="mla_prompt",
    )(qi_tab, ki_tab, q, k, vt, gattn)


def _mla_sample_kernel(q_ref, kn_ref, vn_ref, ckv_ref, kpe_ref, wukv_ref, gattn_ref, out_ref):
    t = q_ref.shape[0]
    heads = [slice(hd * HEAD_PAD, (hd + 1) * HEAD_PAD) for hd in range(MLA_HEADS)]
    cp = ckv_ref[0].astype(BF16)
    past = cp.shape[0]
    kpe = jnp.concatenate([jnp.zeros((past, QK_NOPE), F32), kpe_ref[0],
                           jnp.zeros((past, HEAD_PAD - QK_NOPE - QK_ROPE), F32)], axis=1).astype(BF16)
    q_all = jnp.concatenate([q_ref[:, hs] for hs in heads], axis=0)
    q_lat = jnp.concatenate([_dot_nt(q_ref[:, hs], wukv_ref[:, hs]) for hs in heads], axis=0)
    s1 = _dot_nt(q_lat.astype(BF16), cp) + _dot_nt(q_all, kpe)
    s2 = jnp.concatenate([_dot_nt(q_ref[:, hs], kn_ref[:, hs]) for hs in heads], axis=0)
    m = jnp.maximum(jnp.max(s1, axis=-1, keepdims=True), jnp.max(s2, axis=-1, keepdims=True))
    p1 = jnp.exp2(s1 - m)
    p2 = jnp.exp2(s2 - m)
    l = jnp.sum(p1, axis=-1, keepdims=True) + jnp.sum(p2, axis=-1, keepdims=True)
    p2 = p2.astype(BF16)
    o_lat = _dot(p1.astype(BF16), cp).astype(BF16)
    o_heads = []
    for hd, hs in enumerate(heads):
        rows = slice(hd * t, (hd + 1) * t)
        w_uv = wukv_ref[:, MLA_PAD + hd * HEAD_PAD:MLA_PAD + (hd + 1) * HEAD_PAD]
        o = _dot(o_lat[rows, :], w_uv) + _dot(p2[rows, :], vn_ref[:, hs])
        o_heads.append(o / l[rows, :])
    _attn_finalize(o_heads, gattn_ref, out_ref)


def _mla_sample_call(q, k, v, ckv_past, kpe_past, w_ukv_ext, gattn, *, t_new):
    n = q.shape[0]
    nbatch, past, _ = ckv_past.shape
    assert n == nbatch * t_new
    rows = pl.BlockSpec((t_new, MLA_PAD), lambda b: (b, 0))
    return pl.pallas_call(
        _mla_sample_kernel,
        grid=(nbatch,),
        in_specs=[rows, rows, rows,
                  pl.BlockSpec((1, past, KV_LORA), lambda b: (b, 0, 0)),
                  pl.BlockSpec((1, past, QK_ROPE), lambda b: (b, 0, 0)),
                  _const_spec(w_ukv_ext.shape), _const_spec(gattn.shape)],
        out_specs=rows,
        out_shape=jax.ShapeDtypeStruct((n, MLA_PAD), BF16),
        compiler_params=pltpu.CompilerParams(dimension_semantics=("arbitrary",),
                                             vmem_limit_bytes=VMEM_LIMIT_BYTES),
        name="mla_sample",
    )(q, k, v, ckv_past, kpe_past, w_ukv_ext, gattn)


def _lru_block(x, gate, conv_prev, h_prev, xe_scr, convw_ref, convb_ref, wg_ref, bg_ref, lam_ref, gl_ref):
    tm = x.shape[0]
    xe_scr[0:SUBLANES, :] = conv_prev
    xe_scr[SUBLANES:SUBLANES + tm, :] = x
    xc = convb_ref[...] + xe_scr[pl.ds(SUBLANES - 3, tm), :] * convw_ref[0:1, :]
    xc = xc + xe_scr[pl.ds(SUBLANES - 2, tm), :] * convw_ref[1:2, :]
    xc = xc + xe_scr[pl.ds(SUBLANES - 1, tm), :] * convw_ref[2:3, :]
    xc = xc + x * convw_ref[3:4, :]
    tail = x[tm - SUBLANES:, :]

    y = _dot(xc.astype(BF16), wg_ref[...]) + bg_ref[...]
    r = jax.nn.sigmoid(y[:, :LRU_WIDTH])
    ig = jax.nn.sigmoid(y[:, LRU_WIDTH:])
    nl = -lam_ref[...]
    softplus = jnp.maximum(nl, 0.0) + jnp.log1p(jnp.exp(-jnp.abs(nl)))
    log_a = -LRU_C * r * softplus
    a = jnp.exp(log_a)
    th = jnp.tanh(log_a)
    one_minus_a2 = -2.0 * th / (1.0 - th)
    b = jnp.sqrt(one_minus_a2) * ig * xc

    sub = lax.broadcasted_iota(jnp.int32, (tm, LRU_WIDTH), 0) & (SUBLANES - 1)
    s = 1
    while s < SUBLANES:
        valid = sub >= s
        a_sh = pltpu.roll(a, s, axis=0)
        b_sh = pltpu.roll(b, s, axis=0)
        b = jnp.where(valid, a * b_sh + b, b)
        a = jnp.where(valid, a * a_sh, a)
        s *= 2
    h_groups = []
    for g in range(tm // SUBLANES):
        lo = g * SUBLANES
        hg = b[lo:lo + SUBLANES, :] + a[lo:lo + SUBLANES, :] * h_prev
        h_groups.append(hg)
        h_prev = hg[SUBLANES - 1:SUBLANES, :]
    h = jnp.concatenate(h_groups, axis=0)
    out = _rms(jax.nn.gelu(gate) * h, gl_ref[...]).astype(BF16)
    return out, tail, h_prev


def _tail_kernel(x1_ref, attn_ref, lru_ref, woa_ref, wol_ref, gx_ref, wmq_ref, mk_ref, mv_ref, wmo_ref,
                 g2_ref, w1_ref, w3_ref, w2_ref, gf_ref, y_ref, *, tm, nb, ff_chunk, n_streams):
    streams = _row_streams(tm, n_streams)
    x2s = [x1_ref[lo:lo + sz, :] + _dot(attn_ref[lo:lo + sz, :], woa_ref[...])
           + _dot(lru_ref[lo:lo + sz, :], wol_ref[...]) for lo, sz in streams]

    hqs = [_rms(x2, gx_ref[...]).astype(BF16) for x2 in x2s]
    qs = [(_dot(hq, wmq_ref[...]) * (MEM_HEAD_DIM ** -0.5)).astype(BF16) for hq in hqs]

    def cross_attend(q, mem_batch):
        o_heads = []
        for hd in range(MEM_HEADS):
            lo = hd * MEM_HEAD_DIM
            s = _dot_nt(q[:, lo:lo + MEM_HEAD_DIM], mk_ref[mem_batch, :, lo:lo + MEM_HEAD_DIM].astype(BF16))
            p = jnp.exp(s - jnp.max(s, axis=-1, keepdims=True))
            l = jnp.sum(p, axis=-1, keepdims=True)
            o_heads.append(_dot(p.astype(BF16), mv_ref[mem_batch, :, lo:lo + MEM_HEAD_DIM].astype(BF16)) / l)
        return jnp.concatenate(o_heads, axis=1)

    if nb == 1:
        os_ = [cross_attend(q, 0) for q in qs]
    else:
        rows = tm // nb
        os_ = [jnp.concatenate([cross_attend(qs[0][b * rows:(b + 1) * rows, :], b) for b in range(nb)], axis=0)]
    x3s = [x2 + _dot(o.astype(BF16), wmo_ref[...]) for x2, o in zip(x2s, os_)]

    x4s = _swiglu_half(x3s, g2_ref, w1_ref, w3_ref, w2_ref, ff_chunk)
    for (lo, sz), x4 in zip(streams, x4s):
        y_ref[lo:lo + sz, :] = _rms(x4, gf_ref[...])


def _tail_call(x1, attn, lru, mk, mv, wts, *, seq_len, tm, nb, ff_chunk):
    n = x1.shape[0]
    assert n % tm == 0 and tm % nb == 0 and mk.shape[0] * seq_len == n
    assert (nb == 1 and seq_len % tm == 0) or nb * seq_len == tm
    row = lambda w: pl.BlockSpec((tm, w), lambda i: (i, 0))
    mem = pl.BlockSpec((nb, N_MEM, D_MODEL), lambda i: ((i * tm) // (seq_len * nb), 0, 0),
                       pipeline_mode=pl.Buffered(1 if n == tm else 2))
    c = lambda name: _const_spec(wts[name].shape)
    w_out_attn = wts['w_out_attn'] if attn.shape[1] == MLA_WIDTH else wts['w_out_attn_pad']
    return pl.pallas_call(
        functools.partial(_tail_kernel, tm=tm, nb=nb, ff_chunk=ff_chunk,
                          n_streams=ROW_STREAMS if nb == 1 and tm % (ROW_STREAMS * 16) == 0 else 1),
        grid=(n // tm,),
        in_specs=[row(D_MODEL), row(attn.shape[1]), row(LRU_WIDTH), _const_spec(w_out_attn.shape),
                  c('w_out_lru'), c('xattn_norm'), c('w_mq'),
                  mem, mem, c('w_mo'), c('ffn2_norm'), c('ffn2_w1'), c('ffn2_w3'), c('ffn2_w2'),
                  c('final_norm')],
        out_specs=row(D_MODEL),
        out_shape=jax.ShapeDtypeStruct((n, D_MODEL), F32),
        compiler_params=pltpu.CompilerParams(dimension_semantics=("arbitrary",),
                                             vmem_limit_bytes=VMEM_LIMIT_BYTES),
        name="tail",
    )(x1, attn, lru, w_out_attn, wts['w_out_lru'], wts['xattn_norm'], wts['w_mq'], mk, mv, wts['w_mo'],
      wts['ffn2_norm'], wts['ffn2_w1'], wts['ffn2_w3'], wts['ffn2_w2'], wts['final_norm'])


def _memkv_kernel(mem_ref, g_ref, wmk_ref, wmv_ref, k_ref, v_ref, kb_ref, vb_ref):
    m = _rms(mem_ref[...], g_ref[...]).astype(BF16)
    k = _dot(m, wmk_ref[...])
    v = _dot(m, wmv_ref[...])
    for hd in range(MEM_HEADS):
        lo = hd * MEM_HEAD_DIM
        k_ref[:, hd, :] = k[:, lo:lo + MEM_HEAD_DIM]
        v_ref[:, hd, :] = v[:, lo:lo + MEM_HEAD_DIM]
    kb_ref[...] = k.astype(BF16)
    vb_ref[...] = v.astype(BF16)


def _memkv_call(mem, wts):
    n = mem.shape[0]
    spec = pl.BlockSpec((n, D_MODEL), lambda i: (0, 0))
    cache_spec = pl.BlockSpec((n, MEM_HEADS, MEM_HEAD_DIM), lambda i: (0, 0, 0))
    cache = jax.ShapeDtypeStruct((n, MEM_HEADS, MEM_HEAD_DIM), F32)
    token_major = jax.ShapeDtypeStruct((n, D_MODEL), BF16)
    return pl.pallas_call(
        _memkv_kernel,
        grid=(1,),
        in_specs=[spec, _const_spec(wts['mem_norm'].shape), _const_spec(wts['w_mk'].shape),
                  _const_spec(wts['w_mv'].shape)],
        out_specs=[cache_spec, cache_spec, spec, spec],
        out_shape=[cache, cache, token_major, token_major],
        compiler_params=pltpu.CompilerParams(dimension_semantics=("arbitrary",),
                                             vmem_limit_bytes=VMEM_LIMIT_BYTES),
        name="memkv",
    )(mem, wts['mem_norm'], wts['w_mk'], wts['w_mv'])


CAST_STEPS = 8


def _cast_kernel(*refs):
    n = len(refs) // 2
    for w_ref, o_ref in zip(refs[:n], refs[n:]):
        o_ref[...] = w_ref[...].astype(BF16)


def _cast_bf16_call(ws):
    specs = []
    for w in ws:
        rows, cols = w.shape
        assert rows % (CAST_STEPS * 16) == 0 and cols % LANES == 0
        specs.append(pl.BlockSpec((rows // CAST_STEPS, cols), lambda i: (i, 0)))
    return pl.pallas_call(
        _cast_kernel,
        grid=(CAST_STEPS,),
        in_specs=specs,
        out_specs=specs,
        out_shape=[jax.ShapeDtypeStruct(w.shape, BF16) for w in ws],
        compiler_params=pltpu.CompilerParams(dimension_semantics=("arbitrary",),
                                             vmem_limit_bytes=VMEM_LIMIT_BYTES),
        name="cast_bf16",
    )(*ws)


def _prep_weights(p):
    w = {}
    rowv = lambda a: a.reshape(1, -1).astype(F32)
    for name in ('ffn1_norm', 'mix_norm', 'q_norm', 'kv_norm', 'xattn_norm', 'ffn2_norm', 'final_norm',
                 'mem_norm', 'lru_out_norm', 'conv_b', 'lru_lambda'):
        w[name] = rowv(p[name])
    big = ('ffn1_w1', 'ffn1_w3', 'ffn1_w2', 'ffn2_w1', 'ffn2_w3', 'ffn2_w2', 'w_mq', 'w_mk', 'w_mv', 'w_mo')
    w.update(zip(big, _cast_bf16_call([p[name] for name in big])))

    half = QK_ROPE // 2
    w_in = p['w_in']
    kpe_cols = w_in[:, ZC_KPE:ZC_KPE + QK_ROPE]
    assert HEAD_PAD == QK_NOPE + 2 * QK_ROPE
    pad_l = jnp.zeros((D_MODEL, QK_NOPE), F32)
    kpe_blk = jnp.concatenate([pad_l, kpe_cols, kpe_cols[:, half:], kpe_cols[:, :half]], axis=1)
    lru_cols = w_in[:, Q_LORA + KV_LORA + QK_ROPE:]
    w['w_in_ext'] = jnp.concatenate([w_in[:, :Q_LORA + KV_LORA], kpe_blk, lru_cols], axis=1).astype(BF16)

    wq = p['w_uq'].reshape(Q_LORA, MLA_HEADS, QK_NOPE + QK_ROPE)
    q_blk = jnp.concatenate([wq, wq[:, :, QK_NOPE + half:], wq[:, :, QK_NOPE:QK_NOPE + half]], axis=2)
    w['w_uq_ext'] = q_blk.reshape(Q_LORA, MLA_PAD).astype(BF16)

    wkv = p['w_ukv'].reshape(KV_LORA, MLA_HEADS, QK_NOPE + V_HEAD)
    zk = jnp.zeros((KV_LORA, MLA_HEADS, HEAD_PAD - QK_NOPE), F32)
    zv = jnp.zeros((KV_LORA, MLA_HEADS, HEAD_PAD - V_HEAD), F32)
    k_part = jnp.concatenate([wkv[:, :, :QK_NOPE], zk], axis=2).reshape(KV_LORA, MLA_PAD)
    v_part = jnp.concatenate([wkv[:, :, QK_NOPE:], zv], axis=2).reshape(KV_LORA, MLA_PAD)
    w['w_ukv_ext'] = jnp.concatenate([k_part, v_part], axis=1).astype(BF16)
    vt = jnp.concatenate([wkv[:, :, QK_NOPE:], jnp.zeros((KV_LORA, MLA_HEADS, VT_HEAD - V_HEAD), F32)], axis=2)
    w['w_uvt_ext'] = vt.reshape(KV_LORA, VT_ROWS).T.astype(BF16)

    inv_freq = ROPE_THETA ** (-jnp.arange(0, QK_ROPE, 2, dtype=F32) / QK_ROPE)
    zl = jnp.zeros((QK_NOPE,), F32)
    zr = jnp.zeros((HEAD_PAD - QK_NOPE - QK_ROPE,), F32)
    w['rope_freq'] = jnp.concatenate([zl, inv_freq, inv_freq, zr]).reshape(1, HEAD_PAD)
    sign = jnp.concatenate([zl, -jnp.ones((half,), F32), jnp.ones((half,), F32), zr])
    keep = jnp.concatenate([jnp.ones((QK_NOPE + QK_ROPE,), F32), zr])
    w['rope_sign'] = jnp.stack([sign, keep])

    g_attn = jnp.concatenate([p['attn_out_norm'].reshape(MLA_HEADS, V_HEAD),
                              jnp.zeros((MLA_HEADS, HEAD_PAD - V_HEAD), F32)], axis=1)
    w['attn_out_norm_pad'] = g_attn.reshape(1, MLA_PAD)
    w['attn_out_norm'] = rowv(p['attn_out_norm'])

    w_out = p['w_out']
    wo_attn = jnp.concatenate([w_out[:MLA_WIDTH].reshape(MLA_HEADS, V_HEAD, D_MODEL),
                               jnp.zeros((MLA_HEADS, HEAD_PAD - V_HEAD, D_MODEL), F32)], axis=1)
    w['w_out_attn_pad'] = wo_attn.reshape(MLA_PAD, D_MODEL).astype(BF16)
    w['w_out_attn'] = w_out[:MLA_WIDTH].astype(BF16)
    w['w_out_lru'] = w_out[MLA_WIDTH:].astype(BF16)

    def block_diag(wb):
        eye = jnp.eye(LRU_BLOCKS, dtype=F32)
        return (eye[:, None, :, None] * wb[:, :, None, :]).reshape(LRU_WIDTH, LRU_WIDTH)

    w['lru_wg'] = jnp.concatenate([block_diag(p['lru_wa']), block_diag(p['lru_wx'])], axis=1).astype(BF16)
    w['lru_bg'] = jnp.concatenate([p['lru_ba'].reshape(1, -1), p['lru_bx'].reshape(1, -1)], axis=1).astype(F32)
    w['conv_w'] = p['conv_w'].astype(F32)
    return w


def _pick_tile(n, cap):
    t = min(n, cap)
    while n % t:
        t //= 2
    return t


def _group(x, mk, mv, wts, *, past, conv_init, h_init):
    nbatch, seq, _ = x.shape
    n = nbatch * seq
    xf = x.reshape(n, D_MODEL)
    tm = _pick_tile(n, 512)
    ff_chunk = D_FF // 2
    pos_base = 0 if past is None else past[0].shape[1]
    x1, q, k, v, ckv, kpe, lru, cstate, hstate = _head_call(
        xf, conv_init, h_init, wts, seq_len=seq, pos_base=pos_base, tm=tm, ff_chunk=ff_chunk,
        v_transposed=past is None)
    if past is None:
        assert nbatch == 1
        tq = _pick_tile(n, 1024)
        attn = _mla_prompt_call(q, k, v, wts['attn_out_norm'], tq=tq, tk=tq)
    else:
        ckv_past, kpe_past = past
        attn = _mla_sample_call(q, k, v, ckv_past, kpe_past, wts['w_ukv_ext'], wts['attn_out_norm_pad'],
                                t_new=seq)
    nb = tm // seq if seq < tm else 1
    y = _tail_call(x1, attn, lru, mk, mv, wts, seq_len=seq, tm=tm, nb=nb, ff_chunk=ff_chunk)
    return (y.reshape(nbatch, seq, D_MODEL), ckv.reshape(nbatch, seq, KV_LORA),
            kpe.reshape(nbatch, seq, QK_ROPE), cstate[:, SUBLANES - (CONV_W - 1):, :], hstate[:, 0, :])


def kernel(x_prompt, x_sample, mem_prompt, cache_mla_ckv, cache_mla_kpe, state_conv, state_lru, cache_mem_k, cache_mem_v, ffn1_norm, ffn1_w1, ffn1_w3, ffn1_w2, mix_norm, w_in, q_norm, w_uq, kv_norm, w_ukv, conv_w, conv_b, lru_wa, lru_ba, lru_wx, lru_bx, lru_lambda, attn_out_norm, lru_out_norm, w_out, mem_norm, xattn_norm, w_mq, w_mk, w_mv, w_mo, ffn2_norm, ffn2_w1, ffn2_w3, ffn2_w2, final_norm):
    depth = ffn1_norm.shape[0]
    assert depth == 1
    params = dict(ffn1_norm=ffn1_norm[0], ffn1_w1=ffn1_w1[0], ffn1_w3=ffn1_w3[0], ffn1_w2=ffn1_w2[0],
                  mix_norm=mix_norm[0], w_in=w_in[0], q_norm=q_norm[0], w_uq=w_uq[0], kv_norm=kv_norm[0],
                  w_ukv=w_ukv[0], conv_w=conv_w[0], conv_b=conv_b[0], lru_wa=lru_wa[0], lru_ba=lru_ba[0],
                  lru_wx=lru_wx[0], lru_bx=lru_bx[0], lru_lambda=lru_lambda[0],
                  attn_out_norm=attn_out_norm[0], lru_out_norm=lru_out_norm[0], w_out=w_out[0],
                  mem_norm=mem_norm[0], xattn_norm=xattn_norm[0], w_mq=w_mq[0], w_mk=w_mk[0], w_mv=w_mv[0],
                  w_mo=w_mo[0], ffn2_norm=ffn2_norm[0], ffn2_w1=ffn2_w1[0], ffn2_w3=ffn2_w3[0],
                  ffn2_w2=ffn2_w2[0], final_norm=final_norm)
    wts = _prep_weights(params)

    bp = x_prompt.shape[0]
    bs = x_sample.shape[0]
    n_mem = mem_prompt.shape[1]

    assert bp == 1
    mk_f, mv_f, mk_b, mv_b = _memkv_call(mem_prompt.reshape(bp * n_mem, D_MODEL), wts)
    zero_state = jnp.zeros((bp, SUBLANES, LRU_WIDTH), F32)
    y_p, ckv_p, kpe_p, conv_p, lru_p = _group(
        x_prompt, mk_b.reshape(bp, n_mem, D_MODEL), mv_b.reshape(bp, n_mem, D_MODEL), wts,
        past=None, conv_init=zero_state, h_init=zero_state)

    conv_init = jnp.pad(state_conv[0], ((0, 0), (SUBLANES - (CONV_W - 1), 0), (0, 0)))
    h_init = jnp.broadcast_to(state_lru[0][:, None, :], (bs, SUBLANES, LRU_WIDTH))
    y_s, ckv_s, kpe_s, conv_s, lru_s = _group(
        x_sample, cache_mem_k[0].reshape(bs, n_mem, D_MODEL), cache_mem_v[0].reshape(bs, n_mem, D_MODEL), wts,
        past=(cache_mla_ckv[0], cache_mla_kpe[0]), conv_init=conv_init, h_init=h_init)

    mem_shape = (1, bp, n_mem, MEM_HEADS, MEM_HEAD_DIM)
    return (y_p, y_s, ckv_p[None], kpe_p[None], conv_p[None], lru_p[None],
            mk_f.reshape(mem_shape), mv_f.reshape(mem_shape),
            ckv_s[None], kpe_s[None], conv_s[None], lru_s[None])
```

```python
import functools

import jax
import jax.numpy as jnp
from jax import lax
from jax.experimental import pallas as pl
from jax.experimental.pallas import tpu as pltpu

F32 = jnp.float32
BF16 = jnp.bfloat16

D_MODEL = 1024
CHUNK = 64
N_MEM = 256
MLA_HEADS = 8
Q_LORA = 384
KV_LORA = 256
QK_NOPE = 64
QK_ROPE = 32
V_HEAD = 64
MLA_WIDTH = MLA_HEADS * V_HEAD
MLA_SCALE = (QK_NOPE + QK_ROPE) ** -0.5
ROPE_THETA = 10000.0
LRU_WIDTH = 512
LRU_BLOCKS = 8
LRU_BLOCK = LRU_WIDTH // LRU_BLOCKS
CONV_W = 4
LRU_C = 8.0
MEM_HEADS = 4
MEM_HEAD_DIM = D_MODEL // MEM_HEADS
D_FF = 2816
EPS = 1e-6

LANES = 128
SUBLANES = 8
HEAD_PAD = LANES
MLA_PAD = MLA_HEADS * HEAD_PAD
VT_HEAD = V_HEAD + 16
VT_ROWS = MLA_HEADS * VT_HEAD
Q_SCALE = MLA_SCALE * 1.4426950408889634
QK_LOOKAHEAD = 2
ROW_STREAMS = 2
RESCALE_THRESHOLD = 64.0
FIRST_BLOCK_FLOOR = 64.0
ZC_Q = 0
ZC_KV = ZC_Q + Q_LORA
ZC_KPE = ZC_KV + KV_LORA
ZC_XBR = ZC_KPE + HEAD_PAD
ZC_GBR = ZC_XBR + LRU_WIDTH
Z_COLS = ZC_GBR + LRU_WIDTH
ROPE_PARTNER_ROLL = HEAD_PAD - QK_ROPE
NEG_BIG = -1e30
VMEM_LIMIT_BYTES = 56 * 1024 * 1024


def _const_spec(shape):
    nd = len(shape)
    return pl.BlockSpec(shape, lambda *_: (0,) * nd, pipeline_mode=pl.Buffered(1))


def _rms(x, g):
    ms = jnp.mean(x * x, axis=-1, keepdims=True)
    return x * lax.rsqrt(ms + EPS) * g


def _dot(a, b):
    return jnp.dot(a, b, preferred_element_type=F32)


def _dot_nt(a, b):
    return lax.dot_general(a, b, (((1,), (1,)), ((), ())), preferred_element_type=F32)


def _swiglu_half(xs, g_ref, w1_ref, w3_ref, w2_ref, ff_chunk):
    hs = [_rms(x, g_ref[...]).astype(BF16) for x in xs]
    accs = [jnp.zeros(x.shape, F32) for x in xs]
    for c in range(D_FF // ff_chunk):
        lo = c * ff_chunk
        ups = [(_dot(h, w1_ref[:, lo:lo + ff_chunk]), _dot(h, w3_ref[:, lo:lo + ff_chunk])) for h in hs]
        gs = [(a * jax.nn.sigmoid(a) * b).astype(BF16) for a, b in ups]
        accs = [acc + _dot(g, w2_ref[lo:lo + ff_chunk, :]) for acc, g in zip(accs, gs)]
    return [x + 0.5 * acc for x, acc in zip(xs, accs)]


def _row_streams(tm, n_streams):
    size = tm // n_streams
    return [(s * size, size) for s in range(n_streams)]


def _head_kernel(x_ref, cinit_ref, hinit_ref, g1_ref, w1_ref, w3_ref, w2_ref, gmix_ref, win_ref, gq_ref,
                 wuq_ref, gkv_ref, wukv_ref, wuvt_ref, freq_ref, sign_ref,
                 convw_ref, convb_ref, wg_ref, bg_ref, lam_ref, gl_ref,
                 x1_ref, q_ref, k_ref, v_ref, ckv_ref, kpe_ref, lru_ref, cstate_ref, hstate_ref,
                 xe_scr, ccar_scr, hcar_scr, zbuf_scr, rope_scr,
                 *, tm, seq_len, pos_base, ff_chunk, v_transposed, n_tiles, pipelined, n_streams):
    step = pl.program_id(0)
    lru_consts = (convw_ref, convb_ref, wg_ref, bg_ref, lam_ref, gl_ref)
    bcast = lambda hrow: jnp.broadcast_to(hrow, (SUBLANES, LRU_WIDTH))

    @pl.when(step == 0)
    def _():
        off = lax.broadcasted_iota(jnp.int32, (tm, HEAD_PAD), 0) & (min(seq_len, tm) - 1)
        ang_off = off.astype(F32) * freq_ref[...]
        rope_scr[0] = jnp.cos(ang_off)
        rope_scr[1] = jnp.sin(ang_off)

    if pipelined:
        @pl.when(step == 0)
        def _():
            zbuf_scr[...] = jnp.zeros(zbuf_scr.shape, F32)
            ccar_scr[...] = jnp.zeros(ccar_scr.shape, F32)
            hcar_scr[...] = jnp.zeros(hcar_scr.shape, F32)

        new_seq = lax.rem(step - 1, seq_len // tm) == 0
        conv_prev = jnp.where(new_seq, cinit_ref[0], ccar_scr[...])
        h_prev = jnp.where(new_seq, hinit_ref[0, 0:1, :], hcar_scr[0:1, :])
        out, tail, h_last = _lru_block(zbuf_scr[:, :LRU_WIDTH], zbuf_scr[:, LRU_WIDTH:], conv_prev, h_prev,
                                       xe_scr, *lru_consts)
        lru_ref[...] = out
        ccar_scr[...] = tail
        hcar_scr[...] = bcast(h_last)
        cstate_ref[0] = tail
        hstate_ref[0] = bcast(h_last)
        tile = jnp.minimum(step, n_tiles - 1)
    else:
        tile = step

    streams = _row_streams(tm, n_streams)
    x1 = _swiglu_half([x_ref[lo:lo + sz, :] for lo, sz in streams], g1_ref, w1_ref, w3_ref, w2_ref, ff_chunk)
    x1 = x1[0] if n_streams == 1 else jnp.concatenate(x1, axis=0)
    x1_ref[...] = x1

    h = _rms(x1, gmix_ref[...]).astype(BF16)
    z = _dot(h, win_ref[...])

    base = pos_base + (((tile * tm) & (seq_len - 1)) if seq_len > tm else 0)
    ang_base = jnp.full((1, HEAD_PAD), base, jnp.int32).astype(F32) * freq_ref[...]
    cos_b = jnp.cos(ang_base)
    sin_b = jnp.sin(ang_base)
    cos = (cos_b * rope_scr[0] - sin_b * rope_scr[1]) * sign_ref[1:2, :]
    sin = (sin_b * rope_scr[0] + cos_b * rope_scr[1]) * sign_ref[0:1, :]

    def rope(blk, c, s):
        return blk * c + pltpu.roll(blk, ROPE_PARTNER_ROLL, axis=1) * s

    cqn = _rms(z[:, ZC_Q:ZC_Q + Q_LORA], gq_ref[...]).astype(BF16)
    qz = _dot(cqn, wuq_ref[...])
    cos_q = cos * Q_SCALE
    sin_q = sin * Q_SCALE
    for hd in range(MLA_HEADS):
        lo = hd * HEAD_PAD
        q_ref[:, lo:lo + HEAD_PAD] = rope(qz[:, lo:lo + HEAD_PAD], cos_q, sin_q).astype(BF16)

    ckv = _rms(z[:, ZC_KV:ZC_KV + KV_LORA], gkv_ref[...])
    ckv_ref[...] = ckv
    kpe_blk = rope(z[:, ZC_KPE:ZC_KPE + HEAD_PAD], cos, sin)
    kpe_ref[...] = kpe_blk[:, QK_NOPE:QK_NOPE + QK_ROPE]

    ckv_b = ckv.astype(BF16)
    kz = _dot(ckv_b, wukv_ref[:, :MLA_PAD])
    for hd in range(MLA_HEADS):
        lo = hd * HEAD_PAD
        k_ref[:, lo:lo + HEAD_PAD] = (kz[:, lo:lo + HEAD_PAD] + kpe_blk).astype(BF16)
    if v_transposed:
        vt = _dot_nt(wuvt_ref[...], ckv_b)
        ridx = lax.broadcasted_iota(jnp.int32, vt.shape, 0)
        is_one = ridx == V_HEAD
        for hd in range(1, MLA_HEADS):
            is_one = is_one | (ridx == hd * VT_HEAD + V_HEAD)
        v_ref[...] = jnp.where(is_one, 1.0, vt).astype(BF16)
    else:
        v_ref[...] = _dot(ckv_b, wukv_ref[:, MLA_PAD:]).astype(BF16)

    xbr = z[:, ZC_XBR:ZC_XBR + LRU_WIDTH]
    gbr = z[:, ZC_GBR:ZC_GBR + LRU_WIDTH]
    if pipelined:
        zbuf_scr[:, :LRU_WIDTH] = xbr
        zbuf_scr[:, LRU_WIDTH:] = gbr
    elif seq_len >= tm:
        @pl.when(lax.rem(step, seq_len // tm) == 0)
        def _():
            ccar_scr[...] = cinit_ref[0]
            hcar_scr[...] = hinit_ref[0]

        out, tail, h_last = _lru_block(xbr, gbr, ccar_scr[...], hcar_scr[0:1, :], xe_scr, *lru_consts)
        ccar_scr[...] = tail
        hcar_scr[...] = bcast(h_last)
        lru_ref[...] = out
        cstate_ref[0] = tail
        hstate_ref[0] = bcast(h_last)
    else:
        for b in range(tm // seq_len):
            lo = b * seq_len
            out, tail, h_last = _lru_block(xbr[lo:lo + seq_len, :], gbr[lo:lo + seq_len, :], cinit_ref[b],
                                           hinit_ref[b, 0:1, :], xe_scr, *lru_consts)
            lru_ref[lo:lo + seq_len, :] = out
            cstate_ref[b] = tail
            hstate_ref[b] = bcast(h_last)


def _head_call(x, conv_init, h_init, wts, *, seq_len, pos_base, tm, ff_chunk, v_transposed):
    n = x.shape[0]
    nseq = n // seq_len
    assert n % tm == 0 and seq_len & (seq_len - 1) == 0 and seq_len >= SUBLANES
    assert seq_len % tm == 0 or tm % seq_len == 0
    nsub = max(1, tm // seq_len)
    n_tiles = n // tm
    pipelined = seq_len > tm
    if pipelined:
        row = lambda w: pl.BlockSpec((tm, w), lambda i: (jnp.minimum(i, n_tiles - 1), 0))
        lru_tile = lambda i: jnp.maximum(i - 1, 0)
        state = pl.BlockSpec((1, SUBLANES, LRU_WIDTH), lambda i: ((lru_tile(i) * tm) // seq_len, 0, 0))
    else:
        row = lambda w: pl.BlockSpec((tm, w), lambda i: (i, 0))
        state = pl.BlockSpec((nsub, SUBLANES, LRU_WIDTH), lambda i: ((i * tm) // (seq_len * nsub), 0, 0))
    consts = [wts['ffn1_norm'], wts['ffn1_w1'], wts['ffn1_w3'], wts['ffn1_w2'], wts['mix_norm'],
              wts['w_in_ext'], wts['q_norm'], wts['w_uq_ext'], wts['kv_norm'], wts['w_ukv_ext'],
              wts['w_uvt_ext'], wts['rope_freq'], wts['rope_sign'],
              wts['conv_w'], wts['conv_b'], wts['lru_wg'], wts['lru_bg'], wts['lru_lambda'],
              wts['lru_out_norm']]
    v_shape = (VT_ROWS, n) if v_transposed else (n, MLA_PAD)
    if v_transposed:
        v_spec = pl.BlockSpec((VT_ROWS, tm), lambda i: (0, jnp.minimum(i, n_tiles - 1)))
    else:
        v_spec = row(MLA_PAD)
    state_shape = jax.ShapeDtypeStruct((nseq, SUBLANES, LRU_WIDTH), F32)
    out_shapes = [
        jax.ShapeDtypeStruct((n, D_MODEL), F32),
        jax.ShapeDtypeStruct((n, MLA_PAD), BF16),
        jax.ShapeDtypeStruct((n, MLA_PAD), BF16),
        jax.ShapeDtypeStruct(v_shape, BF16),
        jax.ShapeDtypeStruct((n, KV_LORA), F32),
        jax.ShapeDtypeStruct((n, QK_ROPE), F32),
        jax.ShapeDtypeStruct((n, LRU_WIDTH), BF16),
        state_shape,
        state_shape,
    ]
    out_specs = [row(s.shape[1]) for s in out_shapes[:7]] + [state, state]
    out_specs[3] = v_spec
    if pipelined:
        out_specs[6] = pl.BlockSpec((tm, LRU_WIDTH), lambda i: (lru_tile(i), 0))
    lru_rows = min(tm, seq_len)
    return pl.pallas_call(
        functools.partial(_head_kernel, tm=tm, seq_len=seq_len, pos_base=pos_base, ff_chunk=ff_chunk,
                          v_transposed=v_transposed, n_tiles=n_tiles, pipelined=pipelined,
                          n_streams=ROW_STREAMS if tm % (ROW_STREAMS * 16) == 0 and tm >= 512 else 1),
        grid=(n_tiles + 1 if pipelined else n_tiles,),
        in_specs=[row(D_MODEL), state, state] + [_const_spec(c.shape) for c in consts],
        out_specs=out_specs,
        out_shape=out_shapes,
        scratch_shapes=[pltpu.VMEM((lru_rows + SUBLANES, LRU_WIDTH), F32),
                        pltpu.VMEM((SUBLANES, LRU_WIDTH), F32),
                        pltpu.VMEM((SUBLANES, LRU_WIDTH), F32),
                        pltpu.VMEM((tm if pipelined else SUBLANES, 2 * LRU_WIDTH), F32),
                        pltpu.VMEM((2, tm, HEAD_PAD), F32)],
        compiler_params=pltpu.CompilerParams(dimension_semantics=("arbitrary",),
                                             vmem_limit_bytes=VMEM_LIMIT_BYTES),
        name="head",
    )(x, conv_init, h_init, *consts)


def _attn_finalize(o_heads, gattn_ref, out_ref):
    ss = o_heads[0] * o_heads[0]
    for o in o_heads[1:]:
        ss = ss + o * o
    ms = jnp.sum(ss, axis=-1, keepdims=True) * (1.0 / MLA_WIDTH)
    inv = lax.rsqrt(ms + EPS)
    for hd, o in enumerate(o_heads):
        lo = hd * HEAD_PAD
        out_ref[:, lo:lo + HEAD_PAD] = (o * inv * gattn_ref[:, lo:lo + HEAD_PAD]).astype(BF16)


def _mla_prompt_step(q_ref, k_ref, vt_ref, m_scr, mx_scr, acc_scr, cur, first, visible, *, rescale,
                     keys=None, queries=None, second_part=False):
    k0, k1 = keys if keys is not None else (0, k_ref.shape[0])
    q0, q1 = queries if queries is not None else (0, q_ref.shape[0])

    def scores(hd):
        lo = hd * HEAD_PAD
        return _dot_nt(k_ref[k0:k1, lo:lo + HEAD_PAD], q_ref[q0:q1, lo:lo + HEAD_PAD])

    pending = [scores(hd) for hd in range(QK_LOOKAHEAD)]
    for hd in range(MLA_HEADS):
        st = pending[hd]
        if hd + QK_LOOKAHEAD < MLA_HEADS:
            pending.append(scores(hd + QK_LOOKAHEAD))
        if visible is not None:
            st = jnp.where(visible, st, NEG_BIG)
        vo = hd * VT_HEAD
        blk_max = jnp.max(st, axis=0, keepdims=True)
        if rescale:
            m_prev = jnp.where(first, NEG_BIG, m_scr[hd:hd + 1, q0:q1])
            m_new = jnp.maximum(m_prev, blk_max)
            alpha = jnp.exp2(m_prev - m_new)
            pt = jnp.exp2(st - m_new).astype(BF16)
            acc_scr[cur, vo:vo + VT_HEAD, q0:q1] = (alpha * acc_scr[cur, vo:vo + VT_HEAD, q0:q1]
                                                    + _dot(vt_ref[vo:vo + VT_HEAD, k0:k1], pt))
            m_scr[hd:hd + 1, q0:q1] = m_new
        else:
            src = 1 - cur if second_part else cur
            mx_scr[hd:hd + 1, q0:q1] = jnp.maximum(mx_scr[hd:hd + 1, q0:q1], blk_max) if second_part else blk_max
            pt = jnp.exp2(st - m_scr[hd:hd + 1, q0:q1]).astype(BF16)
            acc_scr[1 - cur, vo:vo + VT_HEAD, q0:q1] = (acc_scr[src, vo:vo + VT_HEAD, q0:q1]
                                                        + _dot(vt_ref[vo:vo + VT_HEAD, k0:k1], pt))


def _mla_prompt_kernel(qi_ref, ki_ref, q_ref, k_ref, vt_ref, gattn_ref, out_ref,
                       m_scr, mx_scr, acc_scr, cur_scr, *, tq, tk):
    step = pl.program_id(0)
    q_lo = qi_ref[step] * tq
    k_lo = ki_ref[step] * tk
    first = k_lo == 0

    @pl.when(first)
    def _():
        m_scr[...] = jnp.zeros(m_scr.shape, F32)
        acc_scr[0] = jnp.zeros(acc_scr.shape[1:], F32)
        cur_scr[0] = 0

    cur = cur_scr[0]
    fully_visible = k_lo + tk <= q_lo + CHUNK

    def chunk_mask():
        kc = (lax.broadcasted_iota(jnp.int32, (tk, tq), 0) + k_lo) // CHUNK
        qc = (lax.broadcasted_iota(jnp.int32, (tk, tq), 1) + q_lo) // CHUNK
        return kc <= qc

    step_fn = functools.partial(_mla_prompt_step, q_ref, k_ref, vt_ref, m_scr, mx_scr, acc_scr, cur, first)

    @pl.when(fully_visible)
    def _():
        step_fn(None, rescale=False)

    @pl.when(jnp.logical_not(fully_visible))
    def _():
        if tq == tk:
            vis = chunk_mask()
            step_fn(vis[:tk // 2, :], rescale=False, keys=(0, tk // 2))
            step_fn(vis[tk // 2:, tq // 2:], rescale=False, keys=(tk // 2, tk), queries=(tq // 2, tq),
                    second_part=True)
        else:
            step_fn(chunk_mask(), rescale=False)

    excess = mx_scr[...] - m_scr[...]
    redo = (jnp.max(excess) > RESCALE_THRESHOLD) | (first & (jnp.min(excess) < -FIRST_BLOCK_FLOOR))

    @pl.when(redo)
    def _():
        step_fn(chunk_mask(), rescale=True)

    @pl.when(jnp.logical_not(redo))
    def _():
        cur_scr[0] = 1 - cur

    @pl.when(k_lo + tk >= q_lo + tq)
    def _():
        fin = cur_scr[0]
        parts = []
        for hd in range(MLA_HEADS):
            vo = hd * VT_HEAD
            parts.append(acc_scr[fin, vo:vo + V_HEAD, :] / acc_scr[fin, vo + V_HEAD:vo + V_HEAD + 1, :])
        o = jnp.concatenate(parts, axis=0).T
        out_ref[...] = _rms(o, gattn_ref[...]).astype(BF16)


def _mla_prompt_call(q, k, vt, gattn, *, tq, tk):
    n = q.shape[0]
    assert n % tq == 0 and tq % tk == 0 and tk % CHUNK == 0
    pairs = [(qi, ki) for qi in range(n // tq) for ki in range((qi + 1) * tq // tk)]
    qi_tab = jnp.asarray([p[0] for p in pairs], jnp.int32)
    ki_tab = jnp.asarray([p[1] for p in pairs], jnp.int32)
    grid_spec = pltpu.PrefetchScalarGridSpec(
        num_scalar_prefetch=2,
        grid=(len(pairs),),
        in_specs=[pl.BlockSpec((tq, MLA_PAD), lambda s, qi, ki: (qi[s], 0)),
                  pl.BlockSpec((tk, MLA_PAD), lambda s, qi, ki: (ki[s], 0)),
                  pl.BlockSpec((VT_ROWS, tk), lambda s, qi, ki: (0, ki[s])),
                  pl.BlockSpec(gattn.shape, lambda s, qi, ki: (0, 0))],
        out_specs=pl.BlockSpec((tq, MLA_WIDTH), lambda s, qi, ki: (qi[s], 0)),
        scratch_shapes=[pltpu.VMEM((MLA_HEADS, tq), F32),
                        pltpu.VMEM((MLA_HEADS, tq), F32),
                        pltpu.VMEM((2, VT_ROWS, tq), F32),
                        pltpu.SMEM((1,), jnp.int32)],
    )
    return pl.pallas_call(
        functools.partial(_mla_prompt_kernel, tq=tq, tk=tk),
        grid_spec=grid_spec,
        out_shape=jax.ShapeDtypeStruct((n, MLA_WIDTH), BF16),
        compiler_params=pltpu.CompilerParams(dimension_semantics=("arbitrary",),
                                             vmem_limit_bytes=VMEM_LIMIT_BYTES),
        name="mla_prompt",
    )(qi_tab, ki_tab, q, k, vt, gattn)


def _mla_sample_kernel(q_ref, kn_ref, vn_ref, ckv_ref, kpe_ref, wukv_ref, gattn_ref, out_ref):
    t = q_ref.shape[0]
    heads = [slice(hd * HEAD_PAD, (hd + 1) * HEAD_PAD) for hd in range(MLA_HEADS)]
    cp = ckv_ref[0].astype(BF16)
    past = cp.shape[0]
    kpe = jnp.concatenate([jnp.zeros((past, QK_NOPE), F32), kpe_ref[0],
                           jnp.zeros((past, HEAD_PAD - QK_NOPE - QK_ROPE), F32)], axis=1).astype(BF16)
    q_all = jnp.concatenate([q_ref[:, hs] for hs in heads], axis=0)
    q_lat = jnp.concatenate([_dot_nt(q_ref[:, hs], wukv_ref[:, hs]) for hs in heads], axis=0)
    s1 = _dot_nt(q_lat.astype(BF16), cp) + _dot_nt(q_all, kpe)
    s2 = jnp.concatenate([_dot_nt(q_ref[:, hs], kn_ref[:, hs]) for hs in heads], axis=0)
    m = jnp.maximum(jnp.max(s1, axis=-1, keepdims=True), jnp.max(s2, axis=-1, keepdims=True))
    p1 = jnp.exp2(s1 - m)
    p2 = jnp.exp2(s2 - m)
    l = jnp.sum(p1, axis=-1, keepdims=True) + jnp.sum(p2, axis=-1, keepdims=True)
    p2 = p2.astype(BF16)
    o_lat = _dot(p1.astype(BF16), cp).astype(BF16)
    o_heads = []
    for hd, hs in enumerate(heads):
        rows = slice(hd * t, (hd + 1) * t)
        w_uv = wukv_ref[:, MLA_PAD + hd * HEAD_PAD:MLA_PAD + (hd + 1) * HEAD_PAD]
        o = _dot(o_lat[rows, :], w_uv) + _dot(p2[rows, :], vn_ref[:, hs])
        o_heads.append(o / l[rows, :])
    _attn_finalize(o_heads, gattn_ref, out_ref)


def _mla_sample_call(q, k, v, ckv_past, kpe_past, w_ukv_ext, gattn, *, t_new):
    n = q.shape[0]
    nbatch, past, _ = ckv_past.shape
    assert n == nbatch * t_new
    rows = pl.BlockSpec((t_new, MLA_PAD), lambda b: (b, 0))
    return pl.pallas_call(
        _mla_sample_kernel,
        grid=(nbatch,),
        in_specs=[rows, rows, rows,
                  pl.BlockSpec((1, past, KV_LORA), lambda b: (b, 0, 0)),
                  pl.BlockSpec((1, past, QK_ROPE), lambda b: (b, 0, 0)),
                  _const_spec(w_ukv_ext.shape), _const_spec(gattn.shape)],
        out_specs=rows,
        out_shape=jax.ShapeDtypeStruct((n, MLA_PAD), BF16),
        compiler_params=pltpu.CompilerParams(dimension_semantics=("arbitrary",),
                                             vmem_limit_bytes=VMEM_LIMIT_BYTES),
        name="mla_sample",
    )(q, k, v, ckv_past, kpe_past, w_ukv_ext, gattn)


def _lru_block(x, gate, conv_prev, h_prev, xe_scr, convw_ref, convb_ref, wg_ref, bg_ref, lam_ref, gl_ref):
    tm = x.shape[0]
    xe_scr[0:SUBLANES, :] = conv_prev
    xe_scr[SUBLANES:SUBLANES + tm, :] = x
    xc = convb_ref[...] + xe_scr[pl.ds(SUBLANES - 3, tm), :] * convw_ref[0:1, :]
    xc = xc + xe_scr[pl.ds(SUBLANES - 2, tm), :] * convw_ref[1:2, :]
    xc = xc + xe_scr[pl.ds(SUBLANES - 1, tm), :] * convw_ref[2:3, :]
    xc = xc + x * convw_ref[3:4, :]
    tail = x[tm - SUBLANES:, :]

    y = _dot(xc.astype(BF16), wg_ref[...]) + bg_ref[...]
    r = jax.nn.sigmoid(y[:, :LRU_WIDTH])
    ig = jax.nn.sigmoid(y[:, LRU_WIDTH:])
    nl = -lam_ref[...]
    softplus = jnp.maximum(nl, 0.0) + jnp.log1p(jnp.exp(-jnp.abs(nl)))
    log_a = -LRU_C * r * softplus
    a = jnp.exp(log_a)
    th = jnp.tanh(log_a)
    one_minus_a2 = -2.0 * th / (1.0 - th)
    b = jnp.sqrt(one_minus_a2) * ig * xc

    sub = lax.broadcasted_iota(jnp.int32, (tm, LRU_WIDTH), 0) & (SUBLANES - 1)
    s = 1
    while s < SUBLANES:
        valid = sub >= s
        a_sh = pltpu.roll(a, s, axis=0)
        b_sh = pltpu.roll(b, s, axis=0)
        b = jnp.where(valid, a * b_sh + b, b)
        a = jnp.where(valid, a * a_sh, a)
        s *= 2
    h_groups = []
    for g in range(tm // SUBLANES):
        lo = g * SUBLANES
        hg = b[lo:lo + SUBLANES, :] + a[lo:lo + SUBLANES, :] * h_prev
        h_groups.append(hg)
        h_prev = hg[SUBLANES - 1:SUBLANES, :]
    h = jnp.concatenate(h_groups, axis=0)
    out = _rms(jax.nn.gelu(gate) * h, gl_ref[...]).astype(BF16)
    return out, tail, h_prev


def _tail_kernel(x1_ref, attn_ref, lru_ref, woa_ref, wol_ref, gx_ref, wmq_ref, mk_ref, mv_ref, wmo_ref,
                 g2_ref, w1_ref, w3_ref, w2_ref, gf_ref, y_ref, *, tm, nb, ff_chunk, n_streams):
    streams = _row_streams(tm, n_streams)
    x2s = [x1_ref[lo:lo + sz, :] + _dot(attn_ref[lo:lo + sz, :], woa_ref[...])
           + _dot(lru_ref[lo:lo + sz, :], wol_ref[...]) for lo, sz in streams]

    hqs = [_rms(x2, gx_ref[...]).astype(BF16) for x2 in x2s]
    qs = [(_dot(hq, wmq_ref[...]) * (MEM_HEAD_DIM ** -0.5)).astype(BF16) for hq in hqs]

    def cross_attend(q, mem_batch):
        o_heads = []
        for hd in range(MEM_HEADS):
            lo = hd * MEM_HEAD_DIM
            s = _dot_nt(q[:, lo:lo + MEM_HEAD_DIM], mk_ref[mem_batch, :, lo:lo + MEM_HEAD_DIM])
            p = jnp.exp(s - jnp.max(s, axis=-1, keepdims=True))
            l = jnp.sum(p, axis=-1, keepdims=True)
            o_heads.append(_dot(p.astype(BF16), mv_ref[mem_batch, :, lo:lo + MEM_HEAD_DIM]) / l)
        return jnp.concatenate(o_heads, axis=1)

    if nb == 1:
        os_ = [cross_attend(q, 0) for q in qs]
    else:
        rows = tm // nb
        os_ = [jnp.concatenate([cross_attend(qs[0][b * rows:(b + 1) * rows, :], b) for b in range(nb)], axis=0)]
    x3s = [x2 + _dot(o.astype(BF16), wmo_ref[...]) for x2, o in zip(x2s, os_)]

    x4s = _swiglu_half(x3s, g2_ref, w1_ref, w3_ref, w2_ref, ff_chunk)
    for (lo, sz), x4 in zip(streams, x4s):
        y_ref[lo:lo + sz, :] = _rms(x4, gf_ref[...])


def _tail_call(x1, attn, lru, mk, mv, wts, *, seq_len, tm, nb, ff_chunk):
    n = x1.shape[0]
    assert n % tm == 0 and tm % nb == 0 and mk.shape[0] * seq_len == n
    assert (nb == 1 and seq_len % tm == 0) or nb * seq_len == tm
    row = lambda w: pl.BlockSpec((tm, w), lambda i: (i, 0))
    mem = pl.BlockSpec((nb, N_MEM, D_MODEL), lambda i: ((i * tm) // (seq_len * nb), 0, 0))
    c = lambda name: _const_spec(wts[name].shape)
    w_out_attn = wts['w_out_attn'] if attn.shape[1] == MLA_WIDTH else wts['w_out_attn_pad']
    return pl.pallas_call(
        functools.partial(_tail_kernel, tm=tm, nb=nb, ff_chunk=ff_chunk,
                          n_streams=ROW_STREAMS if nb == 1 and tm % (ROW_STREAMS * 16) == 0 else 1),
        grid=(n // tm,),
        in_specs=[row(D_MODEL), row(attn.shape[1]), row(LRU_WIDTH), _const_spec(w_out_attn.shape),
                  c('w_out_lru'), c('xattn_norm'), c('w_mq'),
                  mem, mem, c('w_mo'), c('ffn2_norm'), c('ffn2_w1'), c('ffn2_w3'), c('ffn2_w2'),
                  c('final_norm')],
        out_specs=row(D_MODEL),
        out_shape=jax.ShapeDtypeStruct((n, D_MODEL), F32),
        compiler_params=pltpu.CompilerParams(dimension_semantics=("arbitrary",),
                                             vmem_limit_bytes=VMEM_LIMIT_BYTES),
        name="tail",
    )(x1, attn, lru, w_out_attn, wts['w_out_lru'], wts['xattn_norm'], wts['w_mq'], mk, mv, wts['w_mo'],
      wts['ffn2_norm'], wts['ffn2_w1'], wts['ffn2_w3'], wts['ffn2_w2'], wts['final_norm'])


def _memkv_kernel(mem_ref, g_ref, wmk_ref, wmv_ref, k_ref, v_ref, kb_ref, vb_ref):
    m = _rms(mem_ref[...], g_ref[...]).astype(BF16)
    k = _dot(m, wmk_ref[...])
    v = _dot(m, wmv_ref[...])
    for hd in range(MEM_HEADS):
        lo = hd * MEM_HEAD_DIM
        k_ref[:, hd, :] = k[:, lo:lo + MEM_HEAD_DIM]
        v_ref[:, hd, :] = v[:, lo:lo + MEM_HEAD_DIM]
    kb_ref[...] = k.astype(BF16)
    vb_ref[...] = v.astype(BF16)


def _memkv_call(mem, wts):
    n = mem.shape[0]
    spec = pl.BlockSpec((n, D_MODEL), lambda i: (0, 0))
    cache_spec = pl.BlockSpec((n, MEM_HEADS, MEM_HEAD_DIM), lambda i: (0, 0, 0))
    cache = jax.ShapeDtypeStruct((n, MEM_HEADS, MEM_HEAD_DIM), F32)
    token_major = jax.ShapeDtypeStruct((n, D_MODEL), BF16)
    return pl.pallas_call(
        _memkv_kernel,
        grid=(1,),
        in_specs=[spec, _const_spec(wts['mem_norm'].shape), _const_spec(wts['w_mk'].shape),
                  _const_spec(wts['w_mv'].shape)],
        out_specs=[cache_spec, cache_spec, spec, spec],
        out_shape=[cache, cache, token_major, token_major],
        compiler_params=pltpu.CompilerParams(dimension_semantics=("arbitrary",),
                                             vmem_limit_bytes=VMEM_LIMIT_BYTES),
        name="memkv",
    )(mem, wts['mem_norm'], wts['w_mk'], wts['w_mv'])


CAST_STEPS = 8


def _cast_kernel(*refs):
    n = len(refs) // 2
    for w_ref, o_ref in zip(refs[:n], refs[n:]):
        o_ref[...] = w_ref[...].astype(BF16)


def _cast_bf16_call(ws):
    specs = []
    for w in ws:
        rows, cols = w.shape
        assert rows % (CAST_STEPS * 16) == 0 and cols % LANES == 0
        specs.append(pl.BlockSpec((rows // CAST_STEPS, cols), lambda i: (i, 0)))
    return pl.pallas_call(
        _cast_kernel,
        grid=(CAST_STEPS,),
        in_specs=specs,
        out_specs=specs,
        out_shape=[jax.ShapeDtypeStruct(w.shape, BF16) for w in ws],
        compiler_params=pltpu.CompilerParams(dimension_semantics=("arbitrary",),
                                             vmem_limit_bytes=VMEM_LIMIT_BYTES),
        name="cast_bf16",
    )(*ws)


def _prep_weights(p):
    w = {}
    rowv = lambda a: a.reshape(1, -1).astype(F32)
    for name in ('ffn1_norm', 'mix_norm', 'q_norm', 'kv_norm', 'xattn_norm', 'ffn2_norm', 'final_norm',
                 'mem_norm', 'lru_out_norm', 'conv_b', 'lru_lambda'):
        w[name] = rowv(p[name])
    big = ('ffn1_w1', 'ffn1_w3', 'ffn1_w2', 'ffn2_w1', 'ffn2_w3', 'ffn2_w2', 'w_mq', 'w_mk', 'w_mv', 'w_mo')
    w.update(zip(big, _cast_bf16_call([p[name] for name in big])))

    half = QK_ROPE // 2
    w_in = p['w_in']
    kpe_cols = w_in[:, ZC_KPE:ZC_KPE + QK_ROPE]
    assert HEAD_PAD == QK_NOPE + 2 * QK_ROPE
    pad_l = jnp.zeros((D_MODEL, QK_NOPE), F32)
    kpe_blk = jnp.concatenate([pad_l, kpe_cols, kpe_cols[:, half:], kpe_cols[:, :half]], axis=1)
    lru_cols = w_in[:, Q_LORA + KV_LORA + QK_ROPE:]
    w['w_in_ext'] = jnp.concatenate([w_in[:, :Q_LORA + KV_LORA], kpe_blk, lru_cols], axis=1).astype(BF16)

    wq = p['w_uq'].reshape(Q_LORA, MLA_HEADS, QK_NOPE + QK_ROPE)
    q_blk = jnp.concatenate([wq, wq[:, :, QK_NOPE + half:], wq[:, :, QK_NOPE:QK_NOPE + half]], axis=2)
    w['w_uq_ext'] = q_blk.reshape(Q_LORA, MLA_PAD).astype(BF16)

    wkv = p['w_ukv'].reshape(KV_LORA, MLA_HEADS, QK_NOPE + V_HEAD)
    zk = jnp.zeros((KV_LORA, MLA_HEADS, HEAD_PAD - QK_NOPE), F32)
    zv = jnp.zeros((KV_LORA, MLA_HEADS, HEAD_PAD - V_HEAD), F32)
    k_part = jnp.concatenate([wkv[:, :, :QK_NOPE], zk], axis=2).reshape(KV_LORA, MLA_PAD)
    v_part = jnp.concatenate([wkv[:, :, QK_NOPE:], zv], axis=2).reshape(KV_LORA, MLA_PAD)
    w['w_ukv_ext'] = jnp.concatenate([k_part, v_part], axis=1).astype(BF16)
    vt = jnp.concatenate([wkv[:, :, QK_NOPE:], jnp.zeros((KV_LORA, MLA_HEADS, VT_HEAD - V_HEAD), F32)], axis=2)
    w['w_uvt_ext'] = vt.reshape(KV_LORA, VT_ROWS).T.astype(BF16)

    inv_freq = ROPE_THETA ** (-jnp.arange(0, QK_ROPE, 2, dtype=F32) / QK_ROPE)
    zl = jnp.zeros((QK_NOPE,), F32)
    zr = jnp.zeros((HEAD_PAD - QK_NOPE - QK_ROPE,), F32)
    w['rope_freq'] = jnp.concatenate([zl, inv_freq, inv_freq, zr]).reshape(1, HEAD_PAD)
    sign = jnp.concatenate([zl, -jnp.ones((half,), F32), jnp.ones((half,), F32), zr])
    keep = jnp.concatenate([jnp.ones((QK_NOPE + QK_ROPE,), F32), zr])
    w['rope_sign'] = jnp.stack([sign, keep])

    g_attn = jnp.concatenate([p['attn_out_norm'].reshape(MLA_HEADS, V_HEAD),
                              jnp.zeros((MLA_HEADS, HEAD_PAD - V_HEAD), F32)], axis=1)
    w['attn_out_norm_pad'] = g_attn.reshape(1, MLA_PAD)
    w['attn_out_norm'] = rowv(p['attn_out_norm'])

    w_out = p['w_out']
    wo_attn = jnp.concatenate([w_out[:MLA_WIDTH].reshape(MLA_HEADS, V_HEAD, D_MODEL),
                               jnp.zeros((MLA_HEADS, HEAD_PAD - V_HEAD, D_MODEL), F32)], axis=1)
    w['w_out_attn_pad'] = wo_attn.reshape(MLA_PAD, D_MODEL).astype(BF16)
    w['w_out_attn'] = w_out[:MLA_WIDTH].astype(BF16)
    w['w_out_lru'] = w_out[MLA_WIDTH:].astype(BF16)

    def block_diag(wb):
        eye = jnp.eye(LRU_BLOCKS, dtype=F32)
        return (eye[:, None, :, None] * wb[:, :, None, :]).reshape(LRU_WIDTH, LRU_WIDTH)

    w['lru_wg'] = jnp.concatenate([block_diag(p['lru_wa']), block_diag(p['lru_wx'])], axis=1).astype(BF16)
    w['lru_bg'] = jnp.concatenate([p['lru_ba'].reshape(1, -1), p['lru_bx'].reshape(1, -1)], axis=1).astype(F32)
    w['conv_w'] = p['conv_w'].astype(F32)
    return w


def _pick_tile(n, cap):
    t = min(n, cap)
    while n % t:
        t //= 2
    return t


def _group(x, mk, mv, wts, *, past, conv_init, h_init):
    nbatch, seq, _ = x.shape
    n = nbatch * seq
    xf = x.reshape(n, D_MODEL)
    tm = _pick_tile(n, 512)
    ff_chunk = D_FF // 2
    pos_base = 0 if past is None else past[0].shape[1]
    x1, q, k, v, ckv, kpe, lru, cstate, hstate = _head_call(
        xf, conv_init, h_init, wts, seq_len=seq, pos_base=pos_base, tm=tm, ff_chunk=ff_chunk,
        v_transposed=past is None)
    if past is None:
        assert nbatch == 1
        tq = _pick_tile(n, 1024)
        attn = _mla_prompt_call(q, k, v, wts['attn_out_norm'], tq=tq, tk=tq)
    else:
        ckv_past, kpe_past = past
        attn = _mla_sample_call(q, k, v, ckv_past, kpe_past, wts['w_ukv_ext'], wts['attn_out_norm_pad'],
                                t_new=seq)
    nb = tm // seq if seq < tm else 1
    y = _tail_call(x1, attn, lru, mk, mv, wts, seq_len=seq, tm=tm, nb=nb, ff_chunk=ff_chunk)
    return (y.reshape(nbatch, seq, D_MODEL), ckv.reshape(nbatch, seq, KV_LORA),
            kpe.reshape(nbatch, seq, QK_ROPE), cstate[:, SUBLANES - (CONV_W - 1):, :], hstate[:, 0, :])


def kernel(x_prompt, x_sample, mem_prompt, cache_mla_ckv, cache_mla_kpe, state_conv, state_lru, cache_mem_k, cache_mem_v, ffn1_norm, ffn1_w1, ffn1_w3, ffn1_w2, mix_norm, w_in, q_norm, w_uq, kv_norm, w_ukv, conv_w, conv_b, lru_wa, lru_ba, lru_wx, lru_bx, lru_lambda, attn_out_norm, lru_out_norm, w_out, mem_norm, xattn_norm, w_mq, w_mk, w_mv, w_mo, ffn2_norm, ffn2_w1, ffn2_w3, ffn2_w2, final_norm):
    depth = ffn1_norm.shape[0]
    assert depth == 1
    params = dict(ffn1_norm=ffn1_norm[0], ffn1_w1=ffn1_w1[0], ffn1_w3=ffn1_w3[0], ffn1_w2=ffn1_w2[0],
                  mix_norm=mix_norm[0], w_in=w_in[0], q_norm=q_norm[0], w_uq=w_uq[0], kv_norm=kv_norm[0],
                  w_ukv=w_ukv[0], conv_w=conv_w[0], conv_b=conv_b[0], lru_wa=lru_wa[0], lru_ba=lru_ba[0],
                  lru_wx=lru_wx[0], lru_bx=lru_bx[0], lru_lambda=lru_lambda[0],
                  attn_out_norm=attn_out_norm[0], lru_out_norm=lru_out_norm[0], w_out=w_out[0],
                  mem_norm=mem_norm[0], xattn_norm=xattn_norm[0], w_mq=w_mq[0], w_mk=w_mk[0], w_mv=w_mv[0],
                  w_mo=w_mo[0], ffn2_norm=ffn2_norm[0], ffn2_w1=ffn2_w1[0], ffn2_w3=ffn2_w3[0],
                  ffn2_w2=ffn2_w2[0], final_norm=final_norm)
    wts = _prep_weights(params)

    bp = x_prompt.shape[0]
    bs = x_sample.shape[0]
    n_mem = mem_prompt.shape[1]

    assert bp == 1
    mk_f, mv_f, mk_b, mv_b = _memkv_call(mem_prompt.reshape(bp * n_mem, D_MODEL), wts)
    zero_state = jnp.zeros((bp, SUBLANES, LRU_WIDTH), F32)
    y_p, ckv_p, kpe_p, conv_p, lru_p = _group(
        x_prompt, mk_b.reshape(bp, n_mem, D_MODEL), mv_b.reshape(bp, n_mem, D_MODEL), wts,
        past=None, conv_init=zero_state, h_init=zero_state)

    conv_init = jnp.pad(state_conv[0], ((0, 0), (SUBLANES - (CONV_W - 1), 0), (0, 0)))
    h_init = jnp.broadcast_to(state_lru[0][:, None, :], (bs, SUBLANES, LRU_WIDTH))
    y_s, ckv_s, kpe_s, conv_s, lru_s = _group(
        x_sample, cache_mem_k[0].reshape(bs, n_mem, D_MODEL).astype(BF16),
        cache_mem_v[0].reshape(bs, n_mem, D_MODEL).astype(BF16), wts,
        past=(cache_mla_ckv[0], cache_mla_kpe[0]), conv_init=conv_init, h_init=h_init)

    mem_shape = (1, bp, n_mem, MEM_HEADS, MEM_HEAD_DIM)
    return (y_p, y_s, ckv_p[None], kpe_p[None], conv_p[None], lru_p[None],
            mk_f.reshape(mem_shape), mv_f.reshape(mem_shape),
            ckv_s[None], kpe_s[None], conv_s[None], lru_s[None])
```
